```python
import math
import jax, jax.numpy as jnp
from jax import lax
import numpy as np

D_MODEL = 4096
BATCH = 2
SEQ = 8192
DEPTH = 1

CHUNK = 64
N_META = 16
HEAD_DIM = 128
Q_BLOCK = 128
MIX_WIDTH = D_MODEL
DIFF_WIDTH = MIX_WIDTH // 2
FOX_WIDTH = MIX_WIDTH - DIFF_WIDTH
DIFF_V_DIM = 2 * HEAD_DIM
N_DIFF_HEADS = DIFF_WIDTH // DIFF_V_DIM
N_FOX_HEADS = FOX_WIDTH // HEAD_DIM
D_FF = 4 * D_MODEL
ROPE_THETA = 10000.0
LN_EPS = 1e-5
SUBLN_EPS = 1e-5
DEEPNORM_ALPHA = (2.0 * DEPTH) ** 0.25
DEEPNORM_BETA = (8.0 * DEPTH) ** -0.25

DQ_OFF = 0
DK_OFF = DQ_OFF + N_DIFF_HEADS * 2 * HEAD_DIM
DV_OFF = DK_OFF + N_DIFF_HEADS * 2 * HEAD_DIM
FQ_OFF = DV_OFF + N_DIFF_HEADS * DIFF_V_DIM
FK_OFF = FQ_OFF + N_FOX_HEADS * HEAD_DIM
FV_OFF = FK_OFF + N_FOX_HEADS * HEAD_DIM
FF_OFF = FV_OFF + N_FOX_HEADS * HEAD_DIM
IN_WIDTH = FF_OFF + N_FOX_HEADS

kernel_name = "hymba_diff_fox_deepnorm_block"


def _layer_norm(h, g, b):
    hf = h.astype(jnp.float32)
    mu = jnp.mean(hf, axis=-1, keepdims=True)
    var = jnp.mean(jnp.square(hf - mu), axis=-1, keepdims=True)
    y = (hf - mu) * lax.rsqrt(var + LN_EPS) * g.astype(jnp.float32) + b.astype(jnp.float32)
    return y.astype(h.dtype)


def _rope_tables(n):
    inv = 1.0 / (ROPE_THETA ** (jnp.arange(0, HEAD_DIM, 2, dtype=jnp.float32) / HEAD_DIM))
    ang = jnp.arange(n, dtype=jnp.float32)[:, None] * inv[None, :]
    ang = jnp.concatenate([ang, ang], axis=-1)
    return jnp.cos(ang), jnp.sin(ang)


def _apply_rope(t, cos, sin):
    t1, t2 = jnp.split(t, 2, axis=-1)
    rot = jnp.concatenate([-t2, t1], axis=-1)
    out = t.astype(jnp.float32) * cos[:, None, :] + rot.astype(jnp.float32) * sin[:, None, :]
    return out.astype(t.dtype)


def _chunk_ids(n):
    pos = jnp.arange(n)
    return jnp.where(pos < N_META, 0, 1 + (pos - N_META) // CHUNK)


def _pad_seq(a, pad):
    return jnp.pad(a, [(0, 0), (0, pad)] + [(0, 0)] * (a.ndim - 2))


def _sweep(block_fn, n_pad, batch):
    out = lax.map(block_fn, jnp.arange(n_pad // Q_BLOCK))
    out = jnp.moveaxis(out, 0, 1)
    return out.reshape(batch, n_pad, out.shape[3], out.shape[4])


def _diff_attention(q, k, v, lam, lambda_init, subln_g):
    B, L = q.shape[0], q.shape[1]
    Lp = -(-L // Q_BLOCK) * Q_BLOCK
    pad = Lp - L
    qp, kp, vp = _pad_seq(q, pad), _pad_seq(k, pad), _pad_seq(v, pad)
    cid = _chunk_ids(Lp)
    key_ok = jnp.arange(Lp) < L
    scale = HEAD_DIM ** -0.5

    def block(bi):
        start = bi * Q_BLOCK
        qb = lax.dynamic_slice_in_dim(qp, start, Q_BLOCK, axis=1)
        qc = lax.dynamic_slice_in_dim(cid, start, Q_BLOCK)
        s = jnp.einsum('bqhcd,bkhcd->bhcqk', qb, kp).astype(jnp.float32) * scale
        mask = (cid[None, :] <= qc[:, None]) & key_ok[None, :]
        s = jnp.where(mask, s, -jnp.inf)
        p = jax.nn.softmax(s, axis=-1)
        a = (p[:, :, 0] - lam * p[:, :, 1]).astype(vp.dtype)
        return jnp.einsum('bhqk,bkhe->bqhe', a, vp)

    o = _sweep(block, Lp, B)[:, :L]
    of = o.astype(jnp.float32)
    of = of * lax.rsqrt(jnp.mean(jnp.square(of), axis=-1, keepdims=True) + SUBLN_EPS)
    of = of * subln_g.astype(jnp.float32) * (1.0 - lambda_init)
    return of.astype(v.dtype).reshape(B, L, -1)


def _forgetting_attention(q, k, v, log_f):
    B, L = q.shape[0], q.shape[1]
    Lp = -(-L // Q_BLOCK) * Q_BLOCK
    pad = Lp - L
    c = jnp.cumsum(log_f, axis=1)
    cT = jnp.transpose(_pad_seq(c, pad), (0, 2, 1))
    qp, kp, vp = _pad_seq(q, pad), _pad_seq(k, pad), _pad_seq(v, pad)
    idx = jnp.arange(Lp)
    scale = HEAD_DIM ** -0.5

    def block(bi):
        start = bi * Q_BLOCK
        qb = lax.dynamic_slice_in_dim(qp, start, Q_BLOCK, axis=1)
        cq = lax.dynamic_slice_in_dim(cT, start, Q_BLOCK, axis=2)
        qi = start + jnp.arange(Q_BLOCK)
        s = jnp.einsum('bqhd,bkhd->bhqk', qb, kp).astype(jnp.float32) * scale
        s = s + (cq[..., :, None] - cT[..., None, :])
        mask = idx[None, :] <= qi[:, None]
        s = jnp.where(mask, s, -jnp.inf)
        p = jax.nn.softmax(s, axis=-1).astype(vp.dtype)
        return jnp.einsum('bhqk,bkhd->bqhd', p, vp)

    o = _sweep(block, Lp, B)[:, :L]
    return o.reshape(B, L, -1)


def setup_inputs(seed: int = 0) -> dict:
    key = jax.random.key(seed)
    ks = jax.random.split(key, 20)
    f32 = jnp.float32
    x = jax.random.normal(ks[0], (BATCH, SEQ, D_MODEL), f32)
    meta_tokens = jax.random.normal(ks[1], (N_META, D_MODEL), f32)
    ln_in_g = 1.0 + 0.02 * jax.random.normal(ks[2], (D_MODEL,), f32)
    ln_in_b = 0.02 * jax.random.normal(ks[3], (D_MODEL,), f32)
    col_scale = jnp.ones((IN_WIDTH,), f32)
    col_scale = col_scale.at[DV_OFF:FQ_OFF].set(DEEPNORM_BETA).at[FV_OFF:FF_OFF].set(DEEPNORM_BETA)
    w_in = jax.random.normal(ks[4], (DEPTH, D_MODEL, IN_WIDTH), f32) * (D_MODEL ** -0.5) * col_scale
    b_forget = 1.0 + 3.0 * jax.random.uniform(ks[5], (DEPTH, N_FOX_HEADS), f32)
    lambda_q1 = 0.1 * jax.random.normal(ks[6], (DEPTH, HEAD_DIM), f32)
    lambda_k1 = 0.1 * jax.random.normal(ks[7], (DEPTH, HEAD_DIM), f32)
    lambda_q2 = 0.1 * jax.random.normal(ks[8], (DEPTH, HEAD_DIM), f32)
    lambda_k2 = 0.1 * jax.random.normal(ks[9], (DEPTH, HEAD_DIM), f32)
    subln_g = 1.0 + 0.02 * jax.random.normal(ks[10], (DEPTH, DIFF_V_DIM), f32)
    w_out = jax.random.normal(ks[11], (DEPTH, MIX_WIDTH, D_MODEL), f32) * (MIX_WIDTH ** -0.5) * DEEPNORM_BETA
    ln_attn_g = 1.0 + 0.02 * jax.random.normal(ks[12], (DEPTH, D_MODEL), f32)
    ln_attn_b = 0.02 * jax.random.normal(ks[13], (DEPTH, D_MODEL), f32)
    w_up = jax.random.normal(ks[14], (DEPTH, D_MODEL, D_FF), f32) * (D_MODEL ** -0.5) * DEEPNORM_BETA
    w_down = jax.random.normal(ks[15], (DEPTH, D_FF, D_MODEL), f32) * (D_FF ** -0.5) * DEEPNORM_BETA
    ln_mlp_g = 1.0 + 0.02 * jax.random.normal(ks[16], (DEPTH, D_MODEL), f32)
    ln_mlp_b = 0.02 * jax.random.normal(ks[17], (DEPTH, D_MODEL), f32)
    return {"x": x, "meta_tokens": meta_tokens, "ln_in_g": ln_in_g, "ln_in_b": ln_in_b,
            "w_in": w_in, "b_forget": b_forget, "lambda_q1": lambda_q1, "lambda_k1": lambda_k1,
            "lambda_q2": lambda_q2, "lambda_k2": lambda_k2, "subln_g": subln_g, "w_out": w_out,
            "ln_attn_g": ln_attn_g, "ln_attn_b": ln_attn_b, "w_up": w_up, "w_down": w_down,
            "ln_mlp_g": ln_mlp_g, "ln_mlp_b": ln_mlp_b}


def reference(x, meta_tokens, ln_in_g, ln_in_b, w_in, b_forget, lambda_q1, lambda_k1,
              lambda_q2, lambda_k2, subln_g, w_out, ln_attn_g, ln_attn_b, w_up, w_down,
              ln_mlp_g, ln_mlp_b):
    B = x.shape[0]
    meta = jnp.broadcast_to(meta_tokens[None].astype(x.dtype), (B, N_META, x.shape[2]))
    h = jnp.concatenate([meta, x], axis=1)
    h = _layer_norm(h, ln_in_g, ln_in_b)
    L = h.shape[1]
    cos, sin = _rope_tables(L)

    for li in range(DEPTH):
        lambda_init = 0.8 - 0.6 * math.exp(-0.3 * li)
        proj = jnp.einsum('bld,de->ble', h, w_in[li])
        dq = _apply_rope(proj[..., DQ_OFF:DK_OFF].reshape(B, L, 2 * N_DIFF_HEADS, HEAD_DIM), cos, sin)
        dk = _apply_rope(proj[..., DK_OFF:DV_OFF].reshape(B, L, 2 * N_DIFF_HEADS, HEAD_DIM), cos, sin)
        dq = dq.reshape(B, L, N_DIFF_HEADS, 2, HEAD_DIM)
        dk = dk.reshape(B, L, N_DIFF_HEADS, 2, HEAD_DIM)
        dv = proj[..., DV_OFF:FQ_OFF].reshape(B, L, N_DIFF_HEADS, DIFF_V_DIM)
        lam = (jnp.exp(jnp.sum(lambda_q1[li].astype(jnp.float32) * lambda_k1[li].astype(jnp.float32)))
               - jnp.exp(jnp.sum(lambda_q2[li].astype(jnp.float32) * lambda_k2[li].astype(jnp.float32)))
               + lambda_init)
        diff_out = _diff_attention(dq, dk, dv, lam, lambda_init, subln_g[li])
        fq = proj[..., FQ_OFF:FK_OFF].reshape(B, L, N_FOX_HEADS, HEAD_DIM)
        fk = proj[..., FK_OFF:FV_OFF].reshape(B, L, N_FOX_HEADS, HEAD_DIM)
        fv = proj[..., FV_OFF:FF_OFF].reshape(B, L, N_FOX_HEADS, HEAD_DIM)
        log_f = jax.nn.log_sigmoid(proj[..., FF_OFF:].astype(jnp.float32)
                                   + b_forget[li].astype(jnp.float32))
        fox_out = _forgetting_attention(fq, fk, fv, log_f)
        mix = jnp.einsum('ble,ed->bld', jnp.concatenate([diff_out, fox_out], axis=-1), w_out[li])
        h = _layer_norm(DEEPNORM_ALPHA * h + mix, ln_attn_g[li], ln_attn_b[li])
        up = jnp.einsum('bld,df->blf', h, w_up[li])
        ff = jnp.einsum('blf,fd->bld', jnp.square(jax.nn.relu(up)), w_down[li])
        h = _layer_norm(DEEPNORM_ALPHA * h + ff, ln_mlp_g[li], ln_mlp_b[li])

    return h[:, N_META:, :]
```

```python
import functools
import math

import jax
import jax.numpy as jnp
from jax import lax
from jax.experimental import pallas as pl
from jax.experimental.pallas import tpu as pltpu

HEAD_DIM = 128
N_META = 16
CHUNK = 64
ROPE_THETA = 10000.0
LN_EPS = 1e-5
SUBLN_EPS = 1e-5

LANES = 128
VMEM_LIMIT_BYTES = 60 * 2**20

META_PAD = LANES
MASKED = -1e30

F32 = jnp.float32
BF16 = jnp.bfloat16


def _tile(dim, target):
    t = min(dim, target)
    while dim % t:
        t //= 2
    return t


def _params(*sem):
    return pltpu.CompilerParams(dimension_semantics=sem, vmem_limit_bytes=VMEM_LIMIT_BYTES)


def _dot(a, b):
    return jnp.dot(a, b, preferred_element_type=F32)


def _dot_nt(a, b):
    return lax.dot_general(a, b, (((1,), (1,)), ((), ())), preferred_element_type=F32)


def _layer_norm(x, g, b):
    mu = jnp.mean(x, axis=-1, keepdims=True)
    xc = x - mu
    var = jnp.mean(xc * xc, axis=-1, keepdims=True)
    return xc * lax.rsqrt(var + LN_EPS) * g + b


def _ln_gate_kernel(x_ref, g_ref, b_ref, wf_ref, bf_ref, h_ref, c_ref, carry_ref, *, tiles_per_seq):
    @pl.when(pl.program_id(0) % tiles_per_seq == 0)
    def _():
        carry_ref[...] = jnp.zeros_like(carry_ref)

    hb = _layer_norm(x_ref[...], g_ref[...], b_ref[...]).astype(BF16)
    h_ref[...] = hb
    z = _dot(hb, wf_ref[...]) + bf_ref[...]
    log_f = jnp.minimum(z, 0.0) - jnp.log1p(jnp.exp(-jnp.abs(z)))
    tm = z.shape[0]
    row = lax.broadcasted_iota(jnp.int32, (tm, tm), 0)
    col = lax.broadcasted_iota(jnp.int32, (tm, tm), 1)
    lower = (row >= col).astype(F32)
    csum = jnp.dot(lower, log_f, precision=lax.Precision.HIGHEST,
                   preferred_element_type=F32) + carry_ref[...]
    c_ref[...] = csum
    carry_ref[...] = csum[tm - 1:tm, :]


def _ln_gate(x2, g, b, wf, bf, *, seq_rows):
    m, d = x2.shape
    tm = _tile(seq_rows, 256)
    return pl.pallas_call(
        functools.partial(_ln_gate_kernel, tiles_per_seq=seq_rows // tm),
        grid=(m // tm,),
        in_specs=[pl.BlockSpec((tm, d), lambda i: (i, 0)),
                  pl.BlockSpec((1, d), lambda i: (0, 0)),
                  pl.BlockSpec((1, d), lambda i: (0, 0)),
                  pl.BlockSpec((d, LANES), lambda i: (0, 0)),
                  pl.BlockSpec((1, LANES), lambda i: (0, 0))],
        out_specs=[pl.BlockSpec((tm, d), lambda i: (i, 0)),
                   pl.BlockSpec((tm, LANES), lambda i: (i, 0))],
        out_shape=[jax.ShapeDtypeStruct((m, d), BF16), jax.ShapeDtypeStruct((m, LANES), F32)],
        scratch_shapes=[pltpu.VMEM((1, LANES), F32)],
        compiler_params=_params("arbitrary"),
        name="ln_gate",
    )(x2, g, b, wf, bf)


def _in_proj_kernel(a_ref, w_ref, cos_ref, sin_ref, o_ref, *, tiles_per_region, scale):
    region = pl.program_id(1) // tiles_per_region
    acc = _dot(a_ref[...], w_ref[...])
    heads = acc.shape[1] // HEAD_DIM

    def rope(mult):
        cos = cos_ref[...] * mult
        sin = sin_ref[...] * mult
        for c in range(heads):
            t = acc[:, c * HEAD_DIM:(c + 1) * HEAD_DIM]
            o_ref[:, c * HEAD_DIM:(c + 1) * HEAD_DIM] = (
                t * cos + pltpu.roll(t, HEAD_DIM // 2, 1) * sin).astype(o_ref.dtype)

    @pl.when(region == 0)
    def _():
        rope(scale)

    @pl.when(region == 1)
    def _():
        rope(1.0)

    @pl.when(region == 3)
    def _():
        o_ref[...] = (acc * scale).astype(o_ref.dtype)

    @pl.when((region == 2) | (region >= 4))
    def _():
        o_ref[...] = acc.astype(o_ref.dtype)


def _in_proj(h, w, cos, sin, *, seq_rows):
    m, d = h.shape
    n = w.shape[1]
    region = n // 6
    tm = _tile(seq_rows, 1024)
    tn = _tile(region, 1024)
    tiles_per_seq = seq_rows // tm
    return pl.pallas_call(
        functools.partial(_in_proj_kernel, tiles_per_region=region // tn, scale=HEAD_DIM ** -0.5),
        grid=(m // tm, n // tn),
        in_specs=[pl.BlockSpec((tm, d), lambda i, j: (i, 0)),
                  pl.BlockSpec((d, tn), lambda i, j: (0, j)),
                  pl.BlockSpec((tm, HEAD_DIM), lambda i, j: (i % tiles_per_seq, 0)),
                  pl.BlockSpec((tm, HEAD_DIM), lambda i, j: (i % tiles_per_seq, 0))],
        out_specs=pl.BlockSpec((tm, tn), lambda i, j: (i, j)),
        out_shape=jax.ShapeDtypeStruct((m, n), BF16),
        compiler_params=_params("arbitrary", "arbitrary"),
        name="in_proj",
    )(h, w, cos, sin)


def _softmax_block(s, v, first, sh_ref, l_ref, acc_ref, row_bias=None):
    smax = jnp.max(s, axis=1, keepdims=True)
    if first:
        shift = smax if row_bias is None else (smax + row_bias) - row_bias
    else:
        old = sh_ref[...]
        if row_bias is None:
            shift = jnp.maximum(old, smax)
        else:
            shift = jnp.maximum(old + row_bias, smax + row_bias) - row_bias
        alpha = jnp.exp(old - shift)
    p = jnp.exp(s - shift)
    psum = jnp.sum(p, axis=1, keepdims=True)
    pv = _dot(p.astype(v.dtype), v)
    if first:
        l_ref[...] = psum
        acc_ref[...] = pv
    else:
        l_ref[...] = alpha * l_ref[...] + psum
        acc_ref[...] = alpha * acc_ref[...] + pv
    sh_ref[...] = shift


def _diff_attn_kernel(lq1_ref, lk1_ref, lq2_ref, lk2_ref, g_ref, q_ref, k_ref, v_ref, km_ref, vm_ref,
                      o_ref, sh_ref, l_ref, acc_ref, *, tq, lambda_init):
    qi = pl.program_id(2)
    meta_ok = lax.broadcasted_iota(jnp.int32, (tq, META_PAD), 1) < N_META
    row = lax.broadcasted_iota(jnp.int32, (tq, tq), 0)
    col = lax.broadcasted_iota(jnp.int32, (tq, tq), 1)
    chunk_ok = (col // CHUNK) <= (row // CHUNK)

    for c in range(2):
        lo, hi = c * HEAD_DIM, (c + 1) * HEAD_DIM
        qc = q_ref[:, lo:hi]
        state = (sh_ref.at[c], l_ref.at[c], acc_ref.at[c])

        s = jnp.where(meta_ok, _dot_nt(qc, km_ref[:, lo:hi]), MASKED)
        _softmax_block(s, vm_ref[...], True, *state)

        def body(kj, carry):
            off = pl.multiple_of(kj * tq, tq)
            s = _dot_nt(qc, k_ref[pl.ds(off, tq), lo:hi])
            _softmax_block(s, v_ref[pl.ds(off, tq), :], False, *state)
            return carry

        lax.fori_loop(0, qi, body, 0)

        off = pl.multiple_of(qi * tq, tq)
        s = jnp.where(chunk_ok, _dot_nt(qc, k_ref[pl.ds(off, tq), lo:hi]), MASKED)
        _softmax_block(s, v_ref[pl.ds(off, tq), :], False, *state)

    lam = (jnp.exp(jnp.sum(lq1_ref[...] * lk1_ref[...], axis=1, keepdims=True))
           - jnp.exp(jnp.sum(lq2_ref[...] * lk2_ref[...], axis=1, keepdims=True))
           + lambda_init)
    o = acc_ref[0] / l_ref[0] - lam * (acc_ref[1] / l_ref[1])
    o = o * lax.rsqrt(jnp.mean(o * o, axis=-1, keepdims=True) + SUBLN_EPS)
    o_ref[...] = (o * g_ref[...] * (1.0 - lambda_init)).astype(o_ref.dtype)


def _diff_attention(proj, proj_meta, lam_vecs, subln_g, *, batch, seq, n_heads, lambda_init):
    ev = 2 * HEAD_DIM
    tq = _tile(seq, 512)
    nq = seq // tq
    k_blk, v_blk = n_heads, 2 * n_heads
    vec = pl.BlockSpec((1, HEAD_DIM), lambda b, h, i: (0, 0))
    return pl.pallas_call(
        functools.partial(_diff_attn_kernel, tq=tq, lambda_init=lambda_init),
        grid=(batch, n_heads, nq),
        in_specs=[vec, vec, vec, vec,
                  pl.BlockSpec((1, ev), lambda b, h, i: (0, 0)),
                  pl.BlockSpec((tq, ev), lambda b, h, i: (b * nq + i, h)),
                  pl.BlockSpec((seq, ev), lambda b, h, i: (b, k_blk + h)),
                  pl.BlockSpec((seq, ev), lambda b, h, i: (b, v_blk + h)),
                  pl.BlockSpec((META_PAD, ev), lambda b, h, i: (0, k_blk + h)),
                  pl.BlockSpec((META_PAD, ev), lambda b, h, i: (0, v_blk + h))],
        out_specs=pl.BlockSpec((tq, ev), lambda b, h, i: (b * nq + i, h)),
        out_shape=jax.ShapeDtypeStruct((batch * seq, n_heads * ev), BF16),
        scratch_shapes=[pltpu.VMEM((2, tq, 1), F32), pltpu.VMEM((2, tq, 1), F32),
                        pltpu.VMEM((2, tq, ev), F32)],
        compiler_params=_params("arbitrary", "arbitrary", "arbitrary"),
        name="diff_attention",
    )(*lam_vecs, subln_g, proj, proj, proj, proj_meta, proj_meta)


def _fox_attn_kernel(q_ref, k_ref, v_ref, km_ref, vm_ref, cq_ref, ck_ref, cm_ref,
                     o_ref, sh_ref, l_ref, acc_ref, *, tq):
    h = pl.program_id(1)
    qi = pl.program_id(2)
    q = q_ref[...]
    lane = lax.broadcasted_iota(jnp.int32, (tq, LANES), 1)
    cq = jnp.sum(jnp.where(lane == h, cq_ref[...], 0.0), axis=1, keepdims=True)
    state = (sh_ref, l_ref, acc_ref)

    cm = cm_ref[0]
    meta_lane = lax.broadcasted_iota(jnp.int32, (1, META_PAD), 1)
    cm_last = jnp.sum(jnp.where(meta_lane == N_META - 1, cm, 0.0), axis=1, keepdims=True)
    meta_ok = lax.broadcasted_iota(jnp.int32, (tq, META_PAD), 1) < N_META
    s = jnp.where(meta_ok, _dot_nt(q, km_ref[...]) - (cm - cm_last), MASKED)
    _softmax_block(s, vm_ref[...], True, *state, row_bias=cq)

    def body(kj, carry):
        off = pl.multiple_of(kj * tq, tq)
        s = _dot_nt(q, k_ref[pl.ds(off, tq), :]) - ck_ref[0, pl.ds(kj, 1), :]
        _softmax_block(s, v_ref[pl.ds(off, tq), :], False, *state, row_bias=cq)
        return carry

    lax.fori_loop(0, qi, body, 0)

    row = lax.broadcasted_iota(jnp.int32, (tq, tq), 0)
    col = lax.broadcasted_iota(jnp.int32, (tq, tq), 1)
    off = pl.multiple_of(qi * tq, tq)
    s = _dot_nt(q, k_ref[pl.ds(off, tq), :]) - ck_ref[0, pl.ds(qi, 1), :]
    _softmax_block(jnp.where(col <= row, s, MASKED), v_ref[pl.ds(off, tq), :], False, *state,
                   row_bias=cq)

    o_ref[...] = (acc_ref[...] / l_ref[...]).astype(o_ref.dtype)


def _fox_attention(proj, proj_meta, c_rows, c_lanes, cm_lanes, *, batch, seq, n_heads):
    tq = c_lanes.shape[-1]
    nq = seq // tq
    q_blk, k_blk, v_blk = 3 * n_heads, 4 * n_heads, 5 * n_heads
    return pl.pallas_call(
        functools.partial(_fox_attn_kernel, tq=tq),
        grid=(batch, n_heads, nq),
        in_specs=[pl.BlockSpec((tq, HEAD_DIM), lambda b, h, i: (b * nq + i, q_blk + h)),
                  pl.BlockSpec((seq, HEAD_DIM), lambda b, h, i: (b, k_blk + h)),
                  pl.BlockSpec((seq, HEAD_DIM), lambda b, h, i: (b, v_blk + h)),
                  pl.BlockSpec((META_PAD, HEAD_DIM), lambda b, h, i: (0, k_blk + h)),
                  pl.BlockSpec((META_PAD, HEAD_DIM), lambda b, h, i: (0, v_blk + h)),
                  pl.BlockSpec((tq, LANES), lambda b, h, i: (b * nq + i, 0)),
                  pl.BlockSpec((1, nq, tq), lambda b, h, i: (b * n_heads + h, 0, 0)),
                  pl.BlockSpec((1, 1, META_PAD), lambda b, h, i: (h, 0, 0))],
        out_specs=pl.BlockSpec((tq, HEAD_DIM), lambda b, h, i: (b * nq + i, h)),
        out_shape=jax.ShapeDtypeStruct((batch * seq, n_heads * HEAD_DIM), BF16),
        scratch_shapes=[pltpu.VMEM((tq, 1), F32), pltpu.VMEM((tq, 1), F32),
                        pltpu.VMEM((tq, HEAD_DIM), F32)],
        compiler_params=_params("arbitrary", "arbitrary", "arbitrary"),
        name="fox_attention",
    )(proj, proj, proj, proj_meta, proj_meta, c_rows, c_lanes, cm_lanes)


def _matmul_kernel(*refs, n_pairs, nk, epilogue):
    a_refs, w_refs = refs[:n_pairs], refs[n_pairs:2 * n_pairs]
    o_ref = refs[2 * n_pairs]
    acc = _dot(a_refs[0][...], w_refs[0][...])
    for a_ref, w_ref in zip(a_refs[1:], w_refs[1:]):
        acc += _dot(a_ref[...], w_ref[...])
    if nk == 1:
        o_ref[...] = epilogue(acc).astype(o_ref.dtype)
        return
    acc_ref = refs[2 * n_pairs + 1]
    k = pl.program_id(2)

    @pl.when(k == 0)
    def _():
        acc_ref[...] = acc

    @pl.when(k > 0)
    def _():
        acc_ref[...] += acc

    @pl.when(k == nk - 1)
    def _():
        o_ref[...] = epilogue(acc_ref[...]).astype(o_ref.dtype)


def _matmul(pairs, out_dtype, epilogue=lambda acc: acc, name="matmul"):
    m, kdim = pairs[0][0].shape
    n = pairs[0][1].shape[1]
    tm, tn, tk = _tile(m, 1024), _tile(n, 1024), _tile(kdim, 4096 // len(pairs))
    nk = kdim // tk
    return pl.pallas_call(
        functools.partial(_matmul_kernel, n_pairs=len(pairs), nk=nk, epilogue=epilogue),
        grid=(m // tm, n // tn, nk),
        in_specs=([pl.BlockSpec((tm, tk), lambda i, j, k: (i, k))] * len(pairs)
                  + [pl.BlockSpec((tk, tn), lambda i, j, k: (k, j))] * len(pairs)),
        out_specs=pl.BlockSpec((tm, tn), lambda i, j, k: (i, j)),
        out_shape=jax.ShapeDtypeStruct((m, n), out_dtype),
        scratch_shapes=[pltpu.VMEM((tm, tn), F32)] if nk > 1 else [],
        compiler_params=_params("arbitrary", "arbitrary", "arbitrary"),
        name=name,
    )(*[a for a, _ in pairs], *[w for _, w in pairs])


def _attn_norm_kernel(x_ref, mix_ref, g0_ref, b0_ref, g_ref, b_ref, h_ref, hb_ref, *, alpha):
    h0 = _layer_norm(x_ref[...], g0_ref[...], b0_ref[...]).astype(x_ref.dtype)
    h1 = _layer_norm(alpha * h0 + mix_ref[...], g_ref[...], b_ref[...])
    h_ref[...] = h1
    hb_ref[...] = h1.astype(BF16)


def _mlp_norm_kernel(h_ref, ff_ref, g_ref, b_ref, o_ref, *, alpha):
    o_ref[...] = _layer_norm(alpha * h_ref[...] + ff_ref[...], g_ref[...], b_ref[...])


def _row_call(kernel_fn, row_inputs, vec_inputs, out_dtypes, name):
    m, d = row_inputs[0].shape
    tm = _tile(m, 256)
    rows = pl.BlockSpec((tm, d), lambda i: (i, 0))
    vec = pl.BlockSpec((1, d), lambda i: (0, 0))
    return pl.pallas_call(
        kernel_fn,
        grid=(m // tm,),
        in_specs=[rows] * len(row_inputs) + [vec] * len(vec_inputs),
        out_specs=[rows] * len(out_dtypes),
        out_shape=[jax.ShapeDtypeStruct((m, d), dt) for dt in out_dtypes],
        compiler_params=_params("arbitrary"),
        name=name,
    )(*row_inputs, *vec_inputs)


def _rope_tables(n):
    inv = 1.0 / (ROPE_THETA ** (jnp.arange(0, HEAD_DIM, 2, dtype=F32) / HEAD_DIM))
    ang = jnp.arange(n, dtype=F32)[:, None] * inv[None, :]
    ang = jnp.concatenate([ang, ang], axis=-1)
    sign = jnp.where(jnp.arange(HEAD_DIM) < HEAD_DIM // 2, -1.0, 1.0).astype(F32)
    return jnp.cos(ang), jnp.sin(ang) * sign


def kernel(x, meta_tokens, ln_in_g, ln_in_b, w_in, b_forget, lambda_q1, lambda_k1, lambda_q2, lambda_k2,
           subln_g, w_out, ln_attn_g, ln_attn_b, w_up, w_down, ln_mlp_g, ln_mlp_b):
    batch, seq, d = x.shape
    depth = w_in.shape[0]
    assert depth == 1, "meta-token outputs are not carried to a next layer"
    half = d // 2
    n_diff, n_fox = half // (2 * HEAD_DIM), half // HEAD_DIM
    ff_off = 6 * half
    assert w_in.shape[2] == ff_off + n_fox and n_fox <= LANES and seq % CHUNK == 0
    alpha = (2.0 * depth) ** 0.25
    lambda_init = 0.8 - 0.6 * math.exp(-0.3 * 0)

    row = lambda v: v.reshape(1, -1).astype(F32)
    w_proj = w_in[0, :, :ff_off].astype(BF16)
    w_gate = jnp.pad(w_in[0, :, ff_off:], ((0, 0), (0, LANES - n_fox))).astype(BF16)
    b_gate = jnp.pad(b_forget[0], (0, LANES - n_fox)).reshape(1, LANES).astype(F32)
    g_in, b_in = row(ln_in_g), row(ln_in_b)

    cos, sin = _rope_tables(N_META + seq)
    pad_rows = lambda t: jnp.pad(t, ((0, META_PAD - N_META), (0, 0)))

    hm, cm = _ln_gate(pad_rows(meta_tokens.astype(F32)), g_in, b_in, w_gate, b_gate, seq_rows=META_PAD)
    proj_meta = _in_proj(hm, w_proj, pad_rows(cos[:N_META]), pad_rows(sin[:N_META]), seq_rows=META_PAD)

    x2 = x.reshape(batch * seq, d)
    h0b, c_rows = _ln_gate(x2, g_in, b_in, w_gate, b_gate, seq_rows=seq)
    proj = _in_proj(h0b, w_proj, cos[N_META:], sin[N_META:], seq_rows=seq)

    diff_out = _diff_attention(
        proj, proj_meta, [row(v[0]) for v in (lambda_q1, lambda_k1, lambda_q2, lambda_k2)],
        row(subln_g[0]), batch=batch, seq=seq, n_heads=n_diff, lambda_init=lambda_init)

    tq = _tile(seq, 512)
    c_lanes = (c_rows.reshape(batch, seq, LANES)[:, :, :n_fox].transpose(0, 2, 1)
               .reshape(batch * n_fox, seq // tq, tq))
    cm_lanes = cm[:, :n_fox].T.reshape(n_fox, 1, META_PAD)
    fox_out = _fox_attention(proj, proj_meta, c_rows, c_lanes, cm_lanes,
                             batch=batch, seq=seq, n_heads=n_fox)

    w_o = w_out[0].astype(BF16)
    mix = _matmul([(diff_out, w_o[:half]), (fox_out, w_o[half:])], F32, name="out_proj")
    h1, h1b = _row_call(functools.partial(_attn_norm_kernel, alpha=alpha), [x2, mix],
                        [g_in, b_in, row(ln_attn_g[0]), row(ln_attn_b[0])], [F32, BF16], "attn_norm")

    act = _matmul([(h1b, w_up[0].astype(BF16))], BF16,
                  epilogue=lambda acc: jnp.square(jnp.maximum(acc, 0.0)), name="mlp_up")
    ff = _matmul([(act, w_down[0].astype(BF16))], F32, name="mlp_down")
    (out,) = _row_call(functools.partial(_mlp_norm_kernel, alpha=alpha), [h1, ff],
                       [row(ln_mlp_g[0]), row(ln_mlp_b[0])], [F32], "mlp_norm")
    return out.reshape(batch, seq, d)
```

```python
import functools
import math

import jax
import jax.numpy as jnp
from jax import lax
from jax.experimental import pallas as pl
from jax.experimental.pallas import tpu as pltpu

HEAD_DIM = 128
N_META = 16
CHUNK = 64
ROPE_THETA = 10000.0
LN_EPS = 1e-5
SUBLN_EPS = 1e-5

LANES = 128
VMEM_LIMIT_BYTES = 60 * 2**20

META_PAD = LANES
MASKED = -1e30
LOG2E = math.log2(math.e)

F32 = jnp.float32
BF16 = jnp.bfloat16


def _tile(dim, target):
    t = min(dim, target)
    while dim % t:
        t //= 2
    return t


def _params(*sem):
    return pltpu.CompilerParams(dimension_semantics=sem, vmem_limit_bytes=VMEM_LIMIT_BYTES)


def _dot(a, b):
    return jnp.dot(a, b, preferred_element_type=F32)


def _dot_nt(a, b):
    return lax.dot_general(a, b, (((1,), (1,)), ((), ())), preferred_element_type=F32)


def _layer_norm(x, g, b):
    mu = jnp.mean(x, axis=-1, keepdims=True)
    xc = x - mu
    var = jnp.mean(xc * xc, axis=-1, keepdims=True)
    return xc * lax.rsqrt(var + LN_EPS) * g + b


def _ln_gate_kernel(x_ref, g_ref, b_ref, wf_ref, bf_ref, h_ref, c_ref, carry_ref, *, tiles_per_seq):
    @pl.when(pl.program_id(0) % tiles_per_seq == 0)
    def _():
        carry_ref[...] = jnp.zeros_like(carry_ref)

    hb = _layer_norm(x_ref[...], g_ref[...], b_ref[...]).astype(BF16)
    h_ref[...] = hb
    z = _dot(hb, wf_ref[...]) + bf_ref[...]
    log_f = jnp.minimum(z, 0.0) - jnp.log1p(jnp.exp(-jnp.abs(z)))
    tm = z.shape[0]
    row = lax.broadcasted_iota(jnp.int32, (tm, tm), 0)
    col = lax.broadcasted_iota(jnp.int32, (tm, tm), 1)
    lower = (row >= col).astype(F32)
    csum = jnp.dot(lower, log_f, precision=lax.Precision.HIGHEST,
                   preferred_element_type=F32) + carry_ref[...]
    c_ref[...] = csum
    carry_ref[...] = csum[tm - 1:tm, :]


def _ln_gate(x2, g, b, wf, bf, *, seq_rows):
    m, d = x2.shape
    tm = _tile(seq_rows, 256)
    return pl.pallas_call(
        functools.partial(_ln_gate_kernel, tiles_per_seq=seq_rows // tm),
        grid=(m // tm,),
        in_specs=[pl.BlockSpec((tm, d), lambda i: (i, 0)),
                  pl.BlockSpec((1, d), lambda i: (0, 0)),
                  pl.BlockSpec((1, d), lambda i: (0, 0)),
                  pl.BlockSpec((d, LANES), lambda i: (0, 0)),
                  pl.BlockSpec((1, LANES), lambda i: (0, 0))],
        out_specs=[pl.BlockSpec((tm, d), lambda i: (i, 0)),
                   pl.BlockSpec((tm, LANES), lambda i: (i, 0))],
        out_shape=[jax.ShapeDtypeStruct((m, d), BF16), jax.ShapeDtypeStruct((m, LANES), F32)],
        scratch_shapes=[pltpu.VMEM((1, LANES), F32)],
        compiler_params=_params("arbitrary"),
        name="ln_gate",
    )(x2, g, b, wf, bf)


def _in_proj_kernel(a_ref, w_ref, cos_ref, sin_ref, o_ref, *, tiles_per_region, scale):
    region = pl.program_id(1) // tiles_per_region
    acc = _dot(a_ref[...], w_ref[...])
    heads = acc.shape[1] // HEAD_DIM

    def rope(mult):
        cos = cos_ref[...] * mult
        sin = sin_ref[...] * mult
        for c in range(heads):
            t = acc[:, c * HEAD_DIM:(c + 1) * HEAD_DIM]
            o_ref[:, c * HEAD_DIM:(c + 1) * HEAD_DIM] = (
                t * cos + pltpu.roll(t, HEAD_DIM // 2, 1) * sin).astype(o_ref.dtype)

    @pl.when(region == 0)
    def _():
        rope(scale)

    @pl.when(region == 1)
    def _():
        rope(1.0)

    @pl.when(region == 3)
    def _():
        o_ref[...] = (acc * scale).astype(o_ref.dtype)

    @pl.when((region == 2) | (region >= 4))
    def _():
        o_ref[...] = acc.astype(o_ref.dtype)


def _in_proj(h, w, cos, sin, *, seq_rows):
    m, d = h.shape
    n = w.shape[1]
    region = n // 6
    tm = _tile(seq_rows, 1024)
    tn = _tile(region, 1024)
    tiles_per_seq = seq_rows // tm
    return pl.pallas_call(
        functools.partial(_in_proj_kernel, tiles_per_region=region // tn,
                          scale=LOG2E * HEAD_DIM ** -0.5),
        grid=(m // tm, n // tn),
        in_specs=[pl.BlockSpec((tm, d), lambda i, j: (i, 0)),
                  pl.BlockSpec((d, tn), lambda i, j: (0, j)),
                  pl.BlockSpec((tm, HEAD_DIM), lambda i, j: (i % tiles_per_seq, 0)),
                  pl.BlockSpec((tm, HEAD_DIM), lambda i, j: (i % tiles_per_seq, 0))],
        out_specs=pl.BlockSpec((tm, tn), lambda i, j: (i, j)),
        out_shape=jax.ShapeDtypeStruct((m, n), BF16),
        compiler_params=_params("arbitrary", "arbitrary"),
        name="in_proj",
    )(h, w, cos, sin)


def _lane_chunks(a):
    return [a[:, c * LANES:(c + 1) * LANES] for c in range(a.shape[1] // LANES)]


def _softmax_step(chains, first):
    staged = []
    for s, v, rb, sh_ref, l_ref, acc_ref in chains:
        chunks = _lane_chunks(s)
        smax = jnp.max(functools.reduce(jnp.maximum, chunks), axis=1, keepdims=True)
        if first:
            shift = jnp.broadcast_to(smax if rb is None else (smax + rb) - rb, (s.shape[0], LANES))
            alpha = None
        else:
            old = sh_ref[...]
            shift = jnp.maximum(old, smax) if rb is None else jnp.maximum(old + rb, smax + rb) - rb
            alpha = jnp.exp2(old - shift)
        sh_ref[...] = shift
        staged.append((chunks, shift, alpha))
    for (s, v, rb, sh_ref, l_ref, acc_ref), (chunks, shift, alpha) in zip(chains, staged):
        p = [jnp.exp2(c - shift) for c in chunks]
        pv = _dot(jnp.concatenate(p, axis=1).astype(v.dtype), v)
        if l_ref is not None:
            psum = jnp.sum(functools.reduce(jnp.add, p), axis=1, keepdims=True)
            l_ref[...] = jnp.broadcast_to(psum, l_ref.shape) if first else alpha * l_ref[...] + psum
        for c in range(pv.shape[1] // LANES):
            cols = slice(c * LANES, (c + 1) * LANES)
            acc_ref[:, cols] = pv[:, cols] if first else alpha * acc_ref[:, cols] + pv[:, cols]


def _diff_attn_kernel(lq1_ref, lk1_ref, lq2_ref, lk2_ref, g_ref, q_ref, k_ref, v_ref, km_ref, vm_ref,
                      o_ref, sh_ref, l_ref, acc_ref, *, tq, lambda_init):
    qi = pl.program_id(2)
    meta_ok = lax.broadcasted_iota(jnp.int32, (tq, META_PAD), 1) < N_META
    row = lax.broadcasted_iota(jnp.int32, (tq, tq), 0)
    col = lax.broadcasted_iota(jnp.int32, (tq, tq), 1)
    chunk_ok = (col // CHUNK) <= (row // CHUNK)
    maps = [slice(c * HEAD_DIM, (c + 1) * HEAD_DIM) for c in range(2)]

    def chains(score_fn, v):
        return [(score_fn(q_ref[:, m], m), v, None, sh_ref.at[c], l_ref.at[c], acc_ref.at[c])
                for c, m in enumerate(maps)]

    _softmax_step(chains(lambda q, m: jnp.where(meta_ok, _dot_nt(q, km_ref[:, m]), MASKED),
                         vm_ref[...]), True)

    def body(kj, carry):
        rows = pl.ds(pl.multiple_of(kj * tq, tq), tq)
        _softmax_step(chains(lambda q, m: _dot_nt(q, k_ref[rows, m]), v_ref[rows, :]), False)
        return carry

    lax.fori_loop(0, qi, body, 0)

    rows = pl.ds(pl.multiple_of(qi * tq, tq), tq)
    _softmax_step(chains(lambda q, m: jnp.where(chunk_ok, _dot_nt(q, k_ref[rows, m]), MASKED),
                         v_ref[rows, :]), False)

    lam = (jnp.exp(jnp.sum(lq1_ref[...] * lk1_ref[...], axis=1, keepdims=True))
           - jnp.exp(jnp.sum(lq2_ref[...] * lk2_ref[...], axis=1, keepdims=True))
           + lambda_init)
    o = jnp.concatenate(
        [acc_ref[0, :, m] / l_ref[0] - lam * (acc_ref[1, :, m] / l_ref[1]) for m in maps], axis=1)
    o = o * lax.rsqrt(jnp.mean(o * o, axis=-1, keepdims=True) + SUBLN_EPS)
    o_ref[...] = (o * g_ref[...] * (1.0 - lambda_init)).astype(o_ref.dtype)


def _diff_attention(proj, proj_meta, lam_vecs, subln_g, *, batch, seq, n_heads, lambda_init):
    ev = 2 * HEAD_DIM
    tq = _tile(seq, 512)
    nq = seq // tq
    k_blk, v_blk = n_heads, 2 * n_heads
    vec = pl.BlockSpec((1, HEAD_DIM), lambda b, h, i: (0, 0))
    return pl.pallas_call(
        functools.partial(_diff_attn_kernel, tq=tq, lambda_init=lambda_init),
        grid=(batch, n_heads, nq),
        in_specs=[vec, vec, vec, vec,
                  pl.BlockSpec((1, ev), lambda b, h, i: (0, 0)),
                  pl.BlockSpec((tq, ev), lambda b, h, i: (b * nq + i, h)),
                  pl.BlockSpec((seq, ev), lambda b, h, i: (b, k_blk + h)),
                  pl.BlockSpec((seq, ev), lambda b, h, i: (b, v_blk + h)),
                  pl.BlockSpec((META_PAD, ev), lambda b, h, i: (0, k_blk + h)),
                  pl.BlockSpec((META_PAD, ev), lambda b, h, i: (0, v_blk + h))],
        out_specs=pl.BlockSpec((tq, ev), lambda b, h, i: (b * nq + i, h)),
        out_shape=jax.ShapeDtypeStruct((batch * seq, n_heads * ev), BF16),
        scratch_shapes=[pltpu.VMEM((2, tq, LANES), F32), pltpu.VMEM((2, tq, LANES), F32),
                        pltpu.VMEM((2, tq, ev), F32)],
        compiler_params=_params("arbitrary", "arbitrary", "arbitrary"),
        name="diff_attention",
    )(*lam_vecs, subln_g, proj, proj, proj, proj_meta, proj_meta)


def _fox_attn_kernel(q_ref, k_ref, v_ref, km_ref, vm_ref, cq_ref, ck_ref, cm_ref,
                     o_ref, sh_ref, acc_ref, *, tq):
    hp = pl.program_id(1)
    qi = pl.program_id(2)
    heads = [slice(c * HEAD_DIM, (c + 1) * HEAD_DIM) for c in range(2)]
    lane = lax.broadcasted_iota(jnp.int32, (tq, LANES), 1)
    cq = [LOG2E * jnp.sum(jnp.where(lane == 2 * hp + c, cq_ref[...], 0.0), axis=1, keepdims=True)
          for c in range(2)]
    ones = jnp.ones((tq, HEAD_DIM), BF16)
    meta_ones = jnp.ones((META_PAD, HEAD_DIM), BF16)

    def chains(score_fn, value_fn, ones_blk):
        return [(score_fn(c, q_ref[:, m], m), jnp.concatenate([value_fn(m), ones_blk], axis=1), cq[c],
                 sh_ref.at[c], None, acc_ref.at[c]) for c, m in enumerate(heads)]

    meta_lane = lax.broadcasted_iota(jnp.int32, (1, META_PAD), 1)
    meta_ok = lax.broadcasted_iota(jnp.int32, (tq, META_PAD), 1) < N_META

    def meta_scores(c, q, m):
        cm = cm_ref[c]
        cm_last = jnp.sum(jnp.where(meta_lane == N_META - 1, cm, 0.0), axis=1, keepdims=True)
        return jnp.where(meta_ok, _dot_nt(q, km_ref[:, m]) - LOG2E * (cm - cm_last), MASKED)

    _softmax_step(chains(meta_scores, lambda m: vm_ref[:, m], meta_ones), True)

    def frame_scores(kj, rows):
        return lambda c, q, m: _dot_nt(q, k_ref[rows, m]) - LOG2E * ck_ref[c, pl.ds(kj, 1), :]

    def body(kj, carry):
        rows = pl.ds(pl.multiple_of(kj * tq, tq), tq)
        _softmax_step(chains(frame_scores(kj, rows), lambda m: v_ref[rows, m], ones), False)
        return carry

    lax.fori_loop(0, qi, body, 0)

    causal = (lax.broadcasted_iota(jnp.int32, (tq, tq), 1)
              <= lax.broadcasted_iota(jnp.int32, (tq, tq), 0))
    rows = pl.ds(pl.multiple_of(qi * tq, tq), tq)
    diag = frame_scores(qi, rows)
    _softmax_step(chains(lambda c, q, m: jnp.where(causal, diag(c, q, m), MASKED),
                         lambda m: v_ref[rows, m], ones), False)

    for c, m in enumerate(heads):
        o_ref[:, m] = (acc_ref[c, :, :HEAD_DIM] / acc_ref[c, :, HEAD_DIM:]).astype(o_ref.dtype)


def _fox_attention(proj, proj_meta, c_rows, c_lanes, cm_lanes, *, batch, seq, n_heads):
    tq = c_lanes.shape[-1]
    nq = seq // tq
    pairs, ev = n_heads // 2, 2 * HEAD_DIM
    q_blk, k_blk, v_blk = 3 * pairs, 4 * pairs, 5 * pairs
    return pl.pallas_call(
        functools.partial(_fox_attn_kernel, tq=tq),
        grid=(batch, pairs, nq),
        in_specs=[pl.BlockSpec((tq, ev), lambda b, h, i: (b * nq + i, q_blk + h)),
                  pl.BlockSpec((seq, ev), lambda b, h, i: (b, k_blk + h)),
                  pl.BlockSpec((seq, ev), lambda b, h, i: (b, v_blk + h)),
                  pl.BlockSpec((META_PAD, ev), lambda b, h, i: (0, k_blk + h)),
                  pl.BlockSpec((META_PAD, ev), lambda b, h, i: (0, v_blk + h)),
                  pl.BlockSpec((tq, LANES), lambda b, h, i: (b * nq + i, 0)),
                  pl.BlockSpec((2, nq, tq), lambda b, h, i: (b * pairs + h, 0, 0)),
                  pl.BlockSpec((2, 1, META_PAD), lambda b, h, i: (h, 0, 0))],
        out_specs=pl.BlockSpec((tq, ev), lambda b, h, i: (b * nq + i, h)),
        out_shape=jax.ShapeDtypeStruct((batch * seq, n_heads * HEAD_DIM), BF16),
        scratch_shapes=[pltpu.VMEM((2, tq, LANES), F32), pltpu.VMEM((2, tq, ev), F32)],
        compiler_params=_params("arbitrary", "arbitrary", "arbitrary"),
        name="fox_attention",
    )(proj, proj, proj, proj_meta, proj_meta, c_rows, c_lanes, cm_lanes)


def _matmul_kernel(*refs, n_pairs, nk, epilogue):
    a_refs, w_refs = refs[:n_pairs], refs[n_pairs:2 * n_pairs]
    o_ref = refs[2 * n_pairs]
    acc = _dot(a_refs[0][...], w_refs[0][...])
    for a_ref, w_ref in zip(a_refs[1:], w_refs[1:]):
        acc += _dot(a_ref[...], w_ref[...])
    if nk == 1:
        o_ref[...] = epilogue(acc).astype(o_ref.dtype)
        return
    acc_ref = refs[2 * n_pairs + 1]
    k = pl.program_id(2)

    @pl.when(k == 0)
    def _():
        acc_ref[...] = acc

    @pl.when(k > 0)
    def _():
        acc_ref[...] += acc

    @pl.when(k == nk - 1)
    def _():
        o_ref[...] = epilogue(acc_ref[...]).astype(o_ref.dtype)


def _matmul(pairs, out_dtype, epilogue=lambda acc: acc, name="matmul"):
    m, kdim = pairs[0][0].shape
    n = pairs[0][1].shape[1]
    tm, tn, tk = _tile(m, 1024), _tile(n, 1024), _tile(kdim, 4096 // len(pairs))
    nk = kdim // tk
    return pl.pallas_call(
        functools.partial(_matmul_kernel, n_pairs=len(pairs), nk=nk, epilogue=epilogue),
        grid=(m // tm, n // tn, nk),
        in_specs=([pl.BlockSpec((tm, tk), lambda i, j, k: (i, k))] * len(pairs)
                  + [pl.BlockSpec((tk, tn), lambda i, j, k: (k, j))] * len(pairs)),
        out_specs=pl.BlockSpec((tm, tn), lambda i, j, k: (i, j)),
        out_shape=jax.ShapeDtypeStruct((m, n), out_dtype),
        scratch_shapes=[pltpu.VMEM((tm, tn), F32)] if nk > 1 else [],
        compiler_params=_params("arbitrary", "arbitrary", "arbitrary"),
        name=name,
    )(*[a for a, _ in pairs], *[w for _, w in pairs])


def _attn_norm_kernel(x_ref, mix_ref, g0_ref, b0_ref, g_ref, b_ref, h_ref, hb_ref, *, alpha):
    h0 = _layer_norm(x_ref[...], g0_ref[...], b0_ref[...]).astype(x_ref.dtype)
    h1 = _layer_norm(alpha * h0 + mix_ref[...], g_ref[...], b_ref[...])
    h_ref[...] = h1
    hb_ref[...] = h1.astype(BF16)


def _mlp_norm_kernel(h_ref, ff_ref, g_ref, b_ref, o_ref, *, alpha):
    o_ref[...] = _layer_norm(alpha * h_ref[...] + ff_ref[...], g_ref[...], b_ref[...])


def _row_call(kernel_fn, row_inputs, vec_inputs, out_dtypes, name):
    m, d = row_inputs[0].shape
    tm = _tile(m, 256)
    rows = pl.BlockSpec((tm, d), lambda i: (i, 0))
    vec = pl.BlockSpec((1, d), lambda i: (0, 0))
    return pl.pallas_call(
        kernel_fn,
        grid=(m // tm,),
        in_specs=[rows] * len(row_inputs) + [vec] * len(vec_inputs),
        out_specs=[rows] * len(out_dtypes),
        out_shape=[jax.ShapeDtypeStruct((m, d), dt) for dt in out_dtypes],
        compiler_params=_params("arbitrary"),
        name=name,
    )(*row_inputs, *vec_inputs)


def _rope_tables(n):
    inv = 1.0 / (ROPE_THETA ** (jnp.arange(0, HEAD_DIM, 2, dtype=F32) / HEAD_DIM))
    ang = jnp.arange(n, dtype=F32)[:, None] * inv[None, :]
    ang = jnp.concatenate([ang, ang], axis=-1)
    sign = jnp.where(jnp.arange(HEAD_DIM) < HEAD_DIM // 2, -1.0, 1.0).astype(F32)
    return jnp.cos(ang), jnp.sin(ang) * sign


def kernel(x, meta_tokens, ln_in_g, ln_in_b, w_in, b_forget, lambda_q1, lambda_k1, lambda_q2, lambda_k2,
           subln_g, w_out, ln_attn_g, ln_attn_b, w_up, w_down, ln_mlp_g, ln_mlp_b):
    batch, seq, d = x.shape
    depth = w_in.shape[0]
    assert depth == 1, "meta-token outputs are not carried to a next layer"
    half = d // 2
    n_diff, n_fox = half // (2 * HEAD_DIM), half // HEAD_DIM
    ff_off = 6 * half
    assert w_in.shape[2] == ff_off + n_fox and n_fox <= LANES and n_fox % 2 == 0 and seq % CHUNK == 0
    alpha = (2.0 * depth) ** 0.25
    lambda_init = 0.8 - 0.6 * math.exp(-0.3 * 0)

    row = lambda v: v.reshape(1, -1).astype(F32)
    w_proj = w_in[0, :, :ff_off].astype(BF16)
    w_gate = jnp.pad(w_in[0, :, ff_off:], ((0, 0), (0, LANES - n_fox))).astype(BF16)
    b_gate = jnp.pad(b_forget[0], (0, LANES - n_fox)).reshape(1, LANES).astype(F32)
    g_in, b_in = row(ln_in_g), row(ln_in_b)

    cos, sin = _rope_tables(N_META + seq)
    pad_rows = lambda t: jnp.pad(t, ((0, META_PAD - N_META), (0, 0)))

    hm, cm = _ln_gate(pad_rows(meta_tokens.astype(F32)), g_in, b_in, w_gate, b_gate, seq_rows=META_PAD)
    proj_meta = _in_proj(hm, w_proj, pad_rows(cos[:N_META]), pad_rows(sin[:N_META]), seq_rows=META_PAD)

    x2 = x.reshape(batch * seq, d)
    h0b, c_rows = _ln_gate(x2, g_in, b_in, w_gate, b_gate, seq_rows=seq)
    proj = _in_proj(h0b, w_proj, cos[N_META:], sin[N_META:], seq_rows=seq)

    diff_out = _diff_attention(
        proj, proj_meta, [row(v[0]) for v in (lambda_q1, lambda_k1, lambda_q2, lambda_k2)],
        row(subln_g[0]), batch=batch, seq=seq, n_heads=n_diff, lambda_init=lambda_init)

    tq = _tile(seq, 512)
    c_lanes = (c_rows.reshape(batch, seq, LANES)[:, :, :n_fox].transpose(0, 2, 1)
               .reshape(batch * n_fox, seq // tq, tq))
    cm_lanes = cm[:, :n_fox].T.reshape(n_fox, 1, META_PAD)
    fox_out = _fox_attention(proj, proj_meta, c_rows, c_lanes, cm_lanes,
                             batch=batch, seq=seq, n_heads=n_fox)

    w_o = w_out[0].astype(BF16)
    mix = _matmul([(diff_out, w_o[:half]), (fox_out, w_o[half:])], F32, name="out_proj")
    h1, h1b = _row_call(functools.partial(_attn_norm_kernel, alpha=alpha), [x2, mix],
                        [g_in, b_in, row(ln_attn_g[0]), row(ln_attn_b[0])], [F32, BF16], "attn_norm")

    act = _matmul([(h1b, w_up[0].astype(BF16))], BF16,
                  epilogue=lambda acc: jnp.square(jnp.maximum(acc, 0.0)), name="mlp_up")
    ff = _matmul([(act, w_down[0].astype(BF16))], F32, name="mlp_down")
    (out,) = _row_call(functools.partial(_mlp_norm_kernel, alpha=alpha), [h1, ff],
                       [row(ln_mlp_g[0]), row(ln_mlp_b[0])], [F32], "mlp_norm")
    return out.reshape(batch, seq, d)
```

```python
import functools
import math

import jax
import jax.numpy as jnp
from jax import lax
from jax.experimental import pallas as pl
from jax.experimental.pallas import tpu as pltpu

HEAD_DIM = 128
N_META = 16
CHUNK = 64
ROPE_THETA = 10000.0
LN_EPS = 1e-5
SUBLN_EPS = 1e-5

LANES = 128
VMEM_LIMIT_BYTES = 60 * 2**20

META_PAD = LANES
MASKED = -1e30
LOG2E = math.log2(math.e)
ATTN_TQ = 1024
ATTN_TK = 512

F32 = jnp.float32
BF16 = jnp.bfloat16


def _tile(dim, target):
    t = min(dim, target)
    while dim % t:
        t //= 2
    return t


def _params(*sem):
    return pltpu.CompilerParams(dimension_semantics=sem, vmem_limit_bytes=VMEM_LIMIT_BYTES)


def _dot(a, b):
    return jnp.dot(a, b, preferred_element_type=F32)


def _dot_nt(a, b):
    return lax.dot_general(a, b, (((1,), (1,)), ((), ())), preferred_element_type=F32)


def _layer_norm(x, g, b):
    mu = jnp.mean(x, axis=-1, keepdims=True)
    xc = x - mu
    var = jnp.mean(xc * xc, axis=-1, keepdims=True)
    return xc * lax.rsqrt(var + LN_EPS) * g + b


def _ln_gate_kernel(x_ref, g_ref, b_ref, wf_ref, bf_ref, h_ref, c_ref, carry_ref, *, tiles_per_seq):
    @pl.when(pl.program_id(0) % tiles_per_seq == 0)
    def _():
        carry_ref[...] = jnp.zeros_like(carry_ref)

    hb = _layer_norm(x_ref[...], g_ref[...], b_ref[...]).astype(BF16)
    h_ref[...] = hb
    z = _dot(hb, wf_ref[...]) + bf_ref[...]
    log_f = jnp.minimum(z, 0.0) - jnp.log1p(jnp.exp(-jnp.abs(z)))
    tm = z.shape[0]
    row = lax.broadcasted_iota(jnp.int32, (tm, tm), 0)
    col = lax.broadcasted_iota(jnp.int32, (tm, tm), 1)
    lower = (row >= col).astype(F32)
    csum = jnp.dot(lower, log_f, precision=lax.Precision.HIGHEST,
                   preferred_element_type=F32) + carry_ref[...]
    c_ref[...] = csum
    carry_ref[...] = csum[tm - 1:tm, :]


def _ln_gate(x2, g, b, wf, bf, *, seq_rows):
    m, d = x2.shape
    tm = _tile(seq_rows, 256)
    return pl.pallas_call(
        functools.partial(_ln_gate_kernel, tiles_per_seq=seq_rows // tm),
        grid=(m // tm,),
        in_specs=[pl.BlockSpec((tm, d), lambda i: (i, 0)),
                  pl.BlockSpec((1, d), lambda i: (0, 0)),
                  pl.BlockSpec((1, d), lambda i: (0, 0)),
                  pl.BlockSpec((d, LANES), lambda i: (0, 0)),
                  pl.BlockSpec((1, LANES), lambda i: (0, 0))],
        out_specs=[pl.BlockSpec((tm, d), lambda i: (i, 0)),
                   pl.BlockSpec((tm, LANES), lambda i: (i, 0))],
        out_shape=[jax.ShapeDtypeStruct((m, d), BF16), jax.ShapeDtypeStruct((m, LANES), F32)],
        scratch_shapes=[pltpu.VMEM((1, LANES), F32)],
        compiler_params=_params("arbitrary"),
        name="ln_gate",
    )(x2, g, b, wf, bf)


def _in_proj_kernel(a_ref, w_ref, cos_ref, sin_ref, o_ref, *, tiles_per_region, scale):
    region = pl.program_id(1) // tiles_per_region
    acc = _dot(a_ref[...], w_ref[...])
    heads = acc.shape[1] // HEAD_DIM

    def rope(mult):
        cos = cos_ref[...] * mult
        sin = sin_ref[...] * mult
        for c in range(heads):
            t = acc[:, c * HEAD_DIM:(c + 1) * HEAD_DIM]
            o_ref[:, c * HEAD_DIM:(c + 1) * HEAD_DIM] = (
                t * cos + pltpu.roll(t, HEAD_DIM // 2, 1) * sin).astype(o_ref.dtype)

    @pl.when(region == 0)
    def _():
        rope(scale)

    @pl.when(region == 1)
    def _():
        rope(1.0)

    @pl.when(region == 3)
    def _():
        o_ref[...] = (acc * scale).astype(o_ref.dtype)

    @pl.when((region == 2) | (region >= 4))
    def _():
        o_ref[...] = acc.astype(o_ref.dtype)


def _in_proj(h, w, cos, sin, *, seq_rows):
    m, d = h.shape
    region = d // 2
    n = 6 * region
    tm = _tile(seq_rows, 1024)
    tn = _tile(region, 1024)
    tiles_per_seq = seq_rows // tm
    return pl.pallas_call(
        functools.partial(_in_proj_kernel, tiles_per_region=region // tn,
                          scale=LOG2E * HEAD_DIM ** -0.5),
        grid=(m // tm, n // tn),
        in_specs=[pl.BlockSpec((tm, d), lambda i, j: (i, 0)),
                  pl.BlockSpec((d, tn), lambda i, j: (0, j)),
                  pl.BlockSpec((tm, HEAD_DIM), lambda i, j: (i % tiles_per_seq, 0)),
                  pl.BlockSpec((tm, HEAD_DIM), lambda i, j: (i % tiles_per_seq, 0))],
        out_specs=pl.BlockSpec((tm, tn), lambda i, j: (i, j)),
        out_shape=jax.ShapeDtypeStruct((m, n), BF16),
        compiler_params=_params("arbitrary", "arbitrary"),
        name="in_proj",
    )(h, w, cos, sin)


def _lane_chunks(a):
    return [a[:, c * LANES:(c + 1) * LANES] for c in range(a.shape[1] // LANES)]


def _softmax_step(chains, first):
    staged = []
    for load_s, v, rb, sh_ref, l_ref, acc_ref in chains:
        smax = jnp.max(functools.reduce(jnp.maximum, load_s()), axis=1, keepdims=True)
        if first:
            shift = jnp.broadcast_to(smax if rb is None else (smax + rb) - rb, sh_ref.shape)
            alpha = None
        else:
            old = sh_ref[...]
            shift = jnp.maximum(old, smax) if rb is None else jnp.maximum(old + rb, smax + rb) - rb
            alpha = jnp.exp2(old - shift)
        sh_ref[...] = shift
        staged.append((shift, alpha))
    for (load_s, v, rb, sh_ref, l_ref, acc_ref), (shift, alpha) in zip(chains, staged):
        p = [jnp.exp2(c - shift) for c in load_s()]
        pv = _dot(jnp.concatenate(p, axis=1).astype(v.dtype), v)
        if l_ref is not None:
            psum = jnp.sum(functools.reduce(jnp.add, p), axis=1, keepdims=True)
            l_ref[...] = jnp.broadcast_to(psum, l_ref.shape) if first else alpha * l_ref[...] + psum
        for c in range(pv.shape[1] // LANES):
            cols = slice(c * LANES, (c + 1) * LANES)
            acc_ref[:, cols] = pv[:, cols] if first else alpha * acc_ref[:, cols] + pv[:, cols]


def _chunks_of(s):
    return lambda: _lane_chunks(s)


def _chunks_at(s_ref, mask=None):
    def load():
        n = s_ref.shape[1] // LANES
        chunks = [s_ref[:, c * LANES:(c + 1) * LANES] for c in range(n)]
        if mask is None:
            return chunks
        return [jnp.where(m, c, MASKED) for m, c in zip(_lane_chunks(mask), chunks)]
    return load


def _key_tile_sweep(qi, tq, tk, park, consume, first_step):
    every, lower = slice(0, tq), slice(tk, tq)
    park(0, 0, every)
    first_step()

    def body(i, carry):
        park(2 * i + 1, 1, every)
        consume(2 * i, 0, every, False)
        park(2 * i + 2, 0, every)
        consume(2 * i + 1, 1, every, False)
        return carry

    lax.fori_loop(0, qi, body, 0)
    park(2 * qi + 1, 1, lower)
    consume(2 * qi, 0, every, True)
    consume(2 * qi + 1, 1, lower, True)


def _diff_attn_kernel(lq1_ref, lk1_ref, lq2_ref, lk2_ref, g_ref, q_ref, k_ref, v_ref, km_ref, vm_ref,
                      o_ref, sh_ref, l_ref, acc_ref, s_ref, *, tq, tk, lambda_init):
    qi = pl.program_id(2)
    maps = [slice(c * HEAD_DIM, (c + 1) * HEAD_DIM) for c in range(2)]
    keys = lambda kj: pl.ds(pl.multiple_of(kj * tk, tk), tk)

    def chains(load_fn, v, rows):
        return [(load_fn(c, m), v, None, sh_ref.at[c, rows], l_ref.at[c, rows], acc_ref.at[c, rows])
                for c, m in enumerate(maps)]

    def park(kj, slot, rows):
        for c, m in enumerate(maps):
            s_ref[slot, c, rows] = _dot_nt(q_ref[rows, m], k_ref[keys(kj), m])

    def consume(kj, slot, rows, masked):
        mask = None
        if masked:
            n = rows.stop - rows.start
            row = lax.broadcasted_iota(jnp.int32, (n, tk), 0) + (tq - n - rows.start)
            col = lax.broadcasted_iota(jnp.int32, (n, tk), 1)
            mask = (col // CHUNK) <= (row // CHUNK)
        _softmax_step(chains(lambda c, m: _chunks_at(s_ref.at[slot, c, rows], mask),
                             v_ref[keys(kj), :], rows), False)

    def first_step():
        meta_ok = lax.broadcasted_iota(jnp.int32, (tq, META_PAD), 1) < N_META
        _softmax_step(chains(lambda c, m: _chunks_of(jnp.where(
            meta_ok, _dot_nt(q_ref[:, m], km_ref[:, m]), MASKED)), vm_ref[...], slice(0, tq)), True)

    _key_tile_sweep(qi, tq, tk, park, consume, first_step)

    lam = (jnp.exp(jnp.sum(lq1_ref[...] * lk1_ref[...], axis=1, keepdims=True))
           - jnp.exp(jnp.sum(lq2_ref[...] * lk2_ref[...], axis=1, keepdims=True))
           + lambda_init)
    o = jnp.concatenate(
        [acc_ref[0, :, m] / l_ref[0] - lam * (acc_ref[1, :, m] / l_ref[1]) for m in maps], axis=1)
    o = o * lax.rsqrt(jnp.mean(o * o, axis=-1, keepdims=True) + SUBLN_EPS)
    o_ref[...] = (o * g_ref[...] * (1.0 - lambda_init)).astype(o_ref.dtype)


def _diff_attention(proj, proj_meta, lam_vecs, subln_g, *, batch, seq, n_heads, lambda_init):
    ev = 2 * HEAD_DIM
    tq, tk = ATTN_TQ, ATTN_TK
    nq = seq // tq
    k_blk, v_blk = n_heads, 2 * n_heads
    vec = pl.BlockSpec((1, HEAD_DIM), lambda b, h, i: (0, 0))
    return pl.pallas_call(
        functools.partial(_diff_attn_kernel, tq=tq, tk=tk, lambda_init=lambda_init),
        grid=(batch, n_heads, nq),
        in_specs=[vec, vec, vec, vec,
                  pl.BlockSpec((1, ev), lambda b, h, i: (0, 0)),
                  pl.BlockSpec((tq, ev), lambda b, h, i: (b * nq + i, h)),
                  pl.BlockSpec((seq, ev), lambda b, h, i: (b, k_blk + h)),
                  pl.BlockSpec((seq, ev), lambda b, h, i: (b, v_blk + h)),
                  pl.BlockSpec((META_PAD, ev), lambda b, h, i: (0, k_blk + h)),
                  pl.BlockSpec((META_PAD, ev), lambda b, h, i: (0, v_blk + h))],
        out_specs=pl.BlockSpec((tq, ev), lambda b, h, i: (b * nq + i, h)),
        out_shape=jax.ShapeDtypeStruct((batch * seq, n_heads * ev), BF16),
        scratch_shapes=[pltpu.VMEM((2, tq, LANES), F32), pltpu.VMEM((2, tq, LANES), F32),
                        pltpu.VMEM((2, tq, ev), F32), pltpu.VMEM((2, 2, tq, tk), F32)],
        compiler_params=_params("arbitrary", "arbitrary", "arbitrary"),
        name="diff_attention",
    )(*lam_vecs, subln_g, proj, proj, proj, proj_meta, proj_meta)


def _fox_attn_kernel(q_ref, k_ref, v_ref, km_ref, vm_ref, cq_ref, ck_ref, cm_ref,
                     o_ref, sh_ref, acc_ref, s_ref, *, tq, tk):
    hp = pl.program_id(1)
    qi = pl.program_id(2)
    heads = [slice(c * HEAD_DIM, (c + 1) * HEAD_DIM) for c in range(2)]
    keys = lambda kj: pl.ds(pl.multiple_of(kj * tk, tk), tk)
    lane = lax.broadcasted_iota(jnp.int32, (tq, LANES), 1)
    cq = [LOG2E * jnp.sum(jnp.where(lane == 2 * hp + c, cq_ref[...], 0.0), axis=1, keepdims=True)
          for c in range(2)]
    ones = jnp.ones((tk, HEAD_DIM), BF16)

    def chains(load_fn, value_fn, ones_blk, rows):
        return [(load_fn(c, m), jnp.concatenate([value_fn(m), ones_blk], axis=1), cq[c][rows],
                 sh_ref.at[c, rows], None, acc_ref.at[c, rows]) for c, m in enumerate(heads)]

    def park(kj, slot, rows):
        for c, m in enumerate(heads):
            s_ref[slot, c, rows] = (_dot_nt(q_ref[rows, m], k_ref[keys(kj), m])
                                    - LOG2E * ck_ref[c, pl.ds(kj, 1), :])

    def consume(kj, slot, rows, masked):
        mask = None
        if masked:
            n = rows.stop - rows.start
            row = lax.broadcasted_iota(jnp.int32, (n, tk), 0) + (tq - n - rows.start)
            col = lax.broadcasted_iota(jnp.int32, (n, tk), 1)
            mask = col <= row
        _softmax_step(chains(lambda c, m: _chunks_at(s_ref.at[slot, c, rows], mask),
                             lambda m: v_ref[keys(kj), m], ones, rows), False)

    def first_step():
        meta_lane = lax.broadcasted_iota(jnp.int32, (1, META_PAD), 1)
        meta_ok = lax.broadcasted_iota(jnp.int32, (tq, META_PAD), 1) < N_META

        def meta_scores(c, m):
            cm = cm_ref[c]
            cm_last = jnp.sum(jnp.where(meta_lane == N_META - 1, cm, 0.0), axis=1, keepdims=True)
            return _chunks_of(jnp.where(
                meta_ok, _dot_nt(q_ref[:, m], km_ref[:, m]) - LOG2E * (cm - cm_last), MASKED))

        _softmax_step(chains(meta_scores, lambda m: vm_ref[:, m], ones[:META_PAD], slice(0, tq)), True)

    _key_tile_sweep(qi, tq, tk, park, consume, first_step)

    for c, m in enumerate(heads):
        o_ref[:, m] = (acc_ref[c, :, :HEAD_DIM] / acc_ref[c, :, HEAD_DIM:]).astype(o_ref.dtype)


def _fox_attention(proj, proj_meta, c_rows, c_lanes, cm_lanes, *, batch, seq, n_heads):
    tq, tk = ATTN_TQ, ATTN_TK
    nq = seq // tq
    pairs, ev = n_heads // 2, 2 * HEAD_DIM
    q_blk, k_blk, v_blk = 3 * pairs, 4 * pairs, 5 * pairs
    return pl.pallas_call(
        functools.partial(_fox_attn_kernel, tq=tq, tk=tk),
        grid=(batch, pairs, nq),
        in_specs=[pl.BlockSpec((tq, ev), lambda b, h, i: (b * nq + i, q_blk + h)),
                  pl.BlockSpec((seq, ev), lambda b, h, i: (b, k_blk + h)),
                  pl.BlockSpec((seq, ev), lambda b, h, i: (b, v_blk + h)),
                  pl.BlockSpec((META_PAD, ev), lambda b, h, i: (0, k_blk + h)),
                  pl.BlockSpec((META_PAD, ev), lambda b, h, i: (0, v_blk + h)),
                  pl.BlockSpec((tq, LANES), lambda b, h, i: (b * nq + i, 0)),
                  pl.BlockSpec((2, seq // tk, tk), lambda b, h, i: (b * pairs + h, 0, 0)),
                  pl.BlockSpec((2, 1, META_PAD), lambda b, h, i: (h, 0, 0))],
        out_specs=pl.BlockSpec((tq, ev), lambda b, h, i: (b * nq + i, h)),
        out_shape=jax.ShapeDtypeStruct((batch * seq, n_heads * HEAD_DIM), BF16),
        scratch_shapes=[pltpu.VMEM((2, tq, LANES), F32), pltpu.VMEM((2, tq, ev), F32),
                        pltpu.VMEM((2, 2, tq, tk), F32)],
        compiler_params=_params("arbitrary", "arbitrary", "arbitrary"),
        name="fox_attention",
    )(proj, proj, proj, proj_meta, proj_meta, c_rows, c_lanes, cm_lanes)


def _matmul_kernel(*refs, n_lhs, nk, epilogue):
    a_refs, w_refs, o_ref = refs[:n_lhs], refs[n_lhs:2 * n_lhs], refs[2 * n_lhs]
    acc = _dot(a_refs[0][...], w_refs[0][...])
    for a_ref, w_ref in zip(a_refs[1:], w_refs[1:]):
        acc += _dot(a_ref[...], w_ref[...])
    if nk == 1:
        o_ref[...] = epilogue(acc).astype(o_ref.dtype)
        return
    acc_ref = refs[2 * n_lhs + 1]
    k = pl.program_id(2)

    @pl.when(k == 0)
    def _():
        acc_ref[...] = acc

    @pl.when(k > 0)
    def _():
        acc_ref[...] += acc

    @pl.when(k == nk - 1)
    def _():
        o_ref[...] = epilogue(acc_ref[...]).astype(o_ref.dtype)


def _matmul(lhs, w, out_dtype, epilogue=lambda acc: acc, name="matmul"):
    m, ka = lhs[0].shape
    n = w.shape[1]
    tm, tn, tk = _tile(m, 1024), _tile(n, 1024), _tile(ka, 4096 // len(lhs))
    nk = ka // tk
    w_spec = lambda p: pl.BlockSpec((tk, tn), lambda i, j, k: (p * nk + k, j))
    return pl.pallas_call(
        functools.partial(_matmul_kernel, n_lhs=len(lhs), nk=nk, epilogue=epilogue),
        grid=(m // tm, n // tn, nk),
        in_specs=([pl.BlockSpec((tm, tk), lambda i, j, k: (i, k))] * len(lhs)
                  + [w_spec(p) for p in range(len(lhs))]),
        out_specs=pl.BlockSpec((tm, tn), lambda i, j, k: (i, j)),
        out_shape=jax.ShapeDtypeStruct((m, n), out_dtype),
        scratch_shapes=[pltpu.VMEM((tm, tn), F32)] if nk > 1 else [],
        compiler_params=_params("arbitrary", "arbitrary", "arbitrary"),
        name=name,
    )(*lhs, *([w] * len(lhs)))


def _attn_norm_kernel(x_ref, mix_ref, g0_ref, b0_ref, g_ref, b_ref, h_ref, hb_ref, *, alpha):
    h0 = _layer_norm(x_ref[...], g0_ref[...], b0_ref[...]).astype(x_ref.dtype)
    h1 = _layer_norm(alpha * h0 + mix_ref[...], g_ref[...], b_ref[...])
    h_ref[...] = h1
    hb_ref[...] = h1.astype(BF16)


def _mlp_norm_kernel(h_ref, ff_ref, g_ref, b_ref, o_ref, *, alpha):
    o_ref[...] = _layer_norm(alpha * h_ref[...] + ff_ref[...], g_ref[...], b_ref[...])


def _row_call(kernel_fn, row_inputs, vec_inputs, out_dtypes, name):
    m, d = row_inputs[0].shape
    tm = _tile(m, 256)
    rows = pl.BlockSpec((tm, d), lambda i: (i, 0))
    vec = pl.BlockSpec((1, d), lambda i: (0, 0))
    return pl.pallas_call(
        kernel_fn,
        grid=(m // tm,),
        in_specs=[rows] * len(row_inputs) + [vec] * len(vec_inputs),
        out_specs=[rows] * len(out_dtypes),
        out_shape=[jax.ShapeDtypeStruct((m, d), dt) for dt in out_dtypes],
        compiler_params=_params("arbitrary"),
        name=name,
    )(*row_inputs, *vec_inputs)


def _rope_tables(n):
    inv = 1.0 / (ROPE_THETA ** (jnp.arange(0, HEAD_DIM, 2, dtype=F32) / HEAD_DIM))
    ang = jnp.arange(n, dtype=F32)[:, None] * inv[None, :]
    ang = jnp.concatenate([ang, ang], axis=-1)
    sign = jnp.where(jnp.arange(HEAD_DIM) < HEAD_DIM // 2, -1.0, 1.0).astype(F32)
    return jnp.cos(ang), jnp.sin(ang) * sign


def kernel(x, meta_tokens, ln_in_g, ln_in_b, w_in, b_forget, lambda_q1, lambda_k1, lambda_q2, lambda_k2,
           subln_g, w_out, ln_attn_g, ln_attn_b, w_up, w_down, ln_mlp_g, ln_mlp_b):
    batch, seq, d = x.shape
    depth = w_in.shape[0]
    assert depth == 1, "meta-token outputs are not carried to a next layer"
    half = d // 2
    n_diff, n_fox = half // (2 * HEAD_DIM), half // HEAD_DIM
    ff_off = 6 * half
    assert w_in.shape[2] == ff_off + n_fox and n_fox <= LANES and n_fox % 2 == 0 and seq % CHUNK == 0
    alpha = (2.0 * depth) ** 0.25
    lambda_init = 0.8 - 0.6 * math.exp(-0.3 * 0)

    row = lambda v: v.reshape(1, -1).astype(F32)
    w_proj = w_in[0].astype(BF16)
    w_gate = jnp.pad(w_proj[:, ff_off:], ((0, 0), (0, LANES - n_fox)))
    b_gate = jnp.pad(b_forget[0], (0, LANES - n_fox)).reshape(1, LANES).astype(F32)
    g_in, b_in = row(ln_in_g), row(ln_in_b)

    cos, sin = _rope_tables(N_META + seq)
    pad_rows = lambda t: jnp.pad(t, ((0, META_PAD - N_META), (0, 0)))

    hm, cm = _ln_gate(pad_rows(meta_tokens.astype(F32)), g_in, b_in, w_gate, b_gate, seq_rows=META_PAD)
    proj_meta = _in_proj(hm, w_proj, pad_rows(cos[:N_META]), pad_rows(sin[:N_META]), seq_rows=META_PAD)

    x2 = x.reshape(batch * seq, d)
    h0b, c_rows = _ln_gate(x2, g_in, b_in, w_gate, b_gate, seq_rows=seq)
    proj = _in_proj(h0b, w_proj, cos[N_META:], sin[N_META:], seq_rows=seq)

    diff_out = _diff_attention(
        proj, proj_meta, [row(v[0]) for v in (lambda_q1, lambda_k1, lambda_q2, lambda_k2)],
        row(subln_g[0]), batch=batch, seq=seq, n_heads=n_diff, lambda_init=lambda_init)

    assert seq % ATTN_TQ == 0 and ATTN_TQ == 2 * ATTN_TK and ATTN_TK % CHUNK == 0
    c_lanes = (c_rows.reshape(batch, seq, LANES)[:, :, :n_fox].transpose(0, 2, 1)
               .reshape(batch * n_fox, seq // ATTN_TK, ATTN_TK))
    cm_lanes = cm[:, :n_fox].T.reshape(n_fox, 1, META_PAD)
    fox_out = _fox_attention(proj, proj_meta, c_rows, c_lanes, cm_lanes,
                             batch=batch, seq=seq, n_heads=n_fox)

    mix = _matmul([diff_out, fox_out], w_out[0].astype(BF16), F32, name="out_proj")
    h1, h1b = _row_call(functools.partial(_attn_norm_kernel, alpha=alpha), [x2, mix],
                        [g_in, b_in, row(ln_attn_g[0]), row(ln_attn_b[0])], [F32, BF16], "attn_norm")

    act = _matmul([h1b], w_up[0].astype(BF16), BF16,
                  epilogue=lambda acc: jnp.square(jnp.maximum(acc, 0.0)), name="mlp_up")
    ff = _matmul([act], w_down[0].astype(BF16), F32, name="mlp_down")
    (out,) = _row_call(functools.partial(_mlp_norm_kernel, alpha=alpha), [h1, ff],
                       [row(ln_mlp_g[0]), row(ln_mlp_b[0])], [F32], "mlp_norm")
    return out.reshape(batch, seq, d)
```

```python
import functools
import math

import jax
import jax.numpy as jnp
from jax import lax
from jax.experimental import pallas as pl
from jax.experimental.pallas import tpu as pltpu

HEAD_DIM = 128
N_META = 16
CHUNK = 64
ROPE_THETA = 10000.0
LN_EPS = 1e-5
SUBLN_EPS = 1e-5

LANES = 128
VMEM_LIMIT_BYTES = 60 * 2**20

META_PAD = LANES
MASKED = -1e30
LOG2E = math.log2(math.e)
ATTN_TQ = 1024
ATTN_TK = 512

F32 = jnp.float32
BF16 = jnp.bfloat16


def _tile(dim, target):
    t = min(dim, target)
    while dim % t:
        t //= 2
    return t


def _params(*sem):
    return pltpu.CompilerParams(dimension_semantics=sem, vmem_limit_bytes=VMEM_LIMIT_BYTES)


def _dot(a, b):
    return jnp.dot(a, b, preferred_element_type=F32)


def _dot_nt(a, b):
    return lax.dot_general(a, b, (((1,), (1,)), ((), ())), preferred_element_type=F32)


def _layer_norm(x, g, b):
    mu = jnp.mean(x, axis=-1, keepdims=True)
    xc = x - mu
    var = jnp.mean(xc * xc, axis=-1, keepdims=True)
    return xc * lax.rsqrt(var + LN_EPS) * g + b


def _ln_gate_kernel(x_ref, g_ref, b_ref, wf_ref, bf_ref, h_ref, c_ref, carry_ref, *, tiles_per_seq):
    @pl.when(pl.program_id(0) % tiles_per_seq == 0)
    def _():
        carry_ref[...] = jnp.zeros_like(carry_ref)

    hb = _layer_norm(x_ref[...], g_ref[...], b_ref[...]).astype(BF16)
    h_ref[...] = hb
    z = _dot(hb, wf_ref[...]) + bf_ref[...]
    log_f = jnp.minimum(z, 0.0) - jnp.log1p(jnp.exp(-jnp.abs(z)))
    tm = z.shape[0]
    row = lax.broadcasted_iota(jnp.int32, (tm, tm), 0)
    col = lax.broadcasted_iota(jnp.int32, (tm, tm), 1)
    lower = (row >= col).astype(F32)
    csum = jnp.dot(lower, log_f, precision=lax.Precision.HIGHEST,
                   preferred_element_type=F32) + carry_ref[...]
    c_ref[...] = csum
    carry_ref[...] = csum[tm - 1:tm, :]


def _ln_gate(x2, g, b, wf, bf, *, seq_rows):
    m, d = x2.shape
    tm = _tile(seq_rows, 256)
    return pl.pallas_call(
        functools.partial(_ln_gate_kernel, tiles_per_seq=seq_rows // tm),
        grid=(m // tm,),
        in_specs=[pl.BlockSpec((tm, d), lambda i: (i, 0)),
                  pl.BlockSpec((1, d), lambda i: (0, 0)),
                  pl.BlockSpec((1, d), lambda i: (0, 0)),
                  pl.BlockSpec((d, LANES), lambda i: (0, 0)),
                  pl.BlockSpec((1, LANES), lambda i: (0, 0))],
        out_specs=[pl.BlockSpec((tm, d), lambda i: (i, 0)),
                   pl.BlockSpec((tm, LANES), lambda i: (i, 0))],
        out_shape=[jax.ShapeDtypeStruct((m, d), BF16), jax.ShapeDtypeStruct((m, LANES), F32)],
        scratch_shapes=[pltpu.VMEM((1, LANES), F32)],
        compiler_params=_params("arbitrary"),
        name="ln_gate",
    )(x2, g, b, wf, bf)


def _in_proj_kernel(a_ref, w_ref, cos_ref, sin_ref, o_ref, *, tiles_per_region, scale):
    region = pl.program_id(1) // tiles_per_region
    acc = _dot(a_ref[...], w_ref[...])
    heads = acc.shape[1] // HEAD_DIM

    def rope(mult):
        cos = cos_ref[...] * mult
        sin = sin_ref[...] * mult
        for c in range(heads):
            t = acc[:, c * HEAD_DIM:(c + 1) * HEAD_DIM]
            o_ref[:, c * HEAD_DIM:(c + 1) * HEAD_DIM] = (
                t * cos + pltpu.roll(t, HEAD_DIM // 2, 1) * sin).astype(o_ref.dtype)

    @pl.when(region == 0)
    def _():
        rope(scale)

    @pl.when(region == 1)
    def _():
        rope(1.0)

    @pl.when(region == 3)
    def _():
        o_ref[...] = (acc * scale).astype(o_ref.dtype)

    @pl.when((region == 2) | (region >= 4))
    def _():
        o_ref[...] = acc.astype(o_ref.dtype)


def _in_proj(h, w, cos, sin, *, seq_rows):
    m, d = h.shape
    region = d // 2
    n = 6 * region
    tm = _tile(seq_rows, 1024)
    tn = _tile(region, 1024)
    tiles_per_seq = seq_rows // tm
    return pl.pallas_call(
        functools.partial(_in_proj_kernel, tiles_per_region=region // tn,
                          scale=LOG2E * HEAD_DIM ** -0.5),
        grid=(m // tm, n // tn),
        in_specs=[pl.BlockSpec((tm, d), lambda i, j: (i, 0)),
                  pl.BlockSpec((d, tn), lambda i, j: (0, j)),
                  pl.BlockSpec((tm, HEAD_DIM), lambda i, j: (i % tiles_per_seq, 0)),
                  pl.BlockSpec((tm, HEAD_DIM), lambda i, j: (i % tiles_per_seq, 0))],
        out_specs=pl.BlockSpec((tm, tn), lambda i, j: (i, j)),
        out_shape=jax.ShapeDtypeStruct((m, n), BF16),
        compiler_params=_params("arbitrary", "arbitrary"),
        name="in_proj",
    )(h, w, cos, sin)


def _lane_chunks(a):
    return [a[:, c * LANES:(c + 1) * LANES] for c in range(a.shape[1] // LANES)]


def _softmax_step(chains, first):
    staged = []
    for load_s, v, rb, sh_ref, l_ref, acc_ref in chains:
        smax = jnp.max(functools.reduce(jnp.maximum, load_s()), axis=1, keepdims=True)
        if first:
            shift = jnp.broadcast_to(smax if rb is None else (smax + rb) - rb, sh_ref.shape)
            alpha = None
        else:
            old = sh_ref[...]
            shift = jnp.maximum(old, smax) if rb is None else jnp.maximum(old + rb, smax + rb) - rb
            alpha = jnp.exp2(old - shift)
        sh_ref[...] = shift
        staged.append((shift, alpha))
    for (load_s, v, rb, sh_ref, l_ref, acc_ref), (shift, alpha) in zip(chains, staged):
        p = [jnp.exp2(c - shift) for c in load_s()]
        pv = _dot(jnp.concatenate(p, axis=1).astype(v.dtype), v)
        if l_ref is not None:
            psum = jnp.sum(functools.reduce(jnp.add, p), axis=1, keepdims=True)
            l_ref[...] = jnp.broadcast_to(psum, l_ref.shape) if first else alpha * l_ref[...] + psum
        for c in range(pv.shape[1] // LANES):
            cols = slice(c * LANES, (c + 1) * LANES)
            acc_ref[:, cols] = pv[:, cols] if first else alpha * acc_ref[:, cols] + pv[:, cols]


def _chunks_of(s):
    return lambda: _lane_chunks(s)


def _chunks_at(s_ref, mask=None):
    def load():
        n = s_ref.shape[1] // LANES
        chunks = [s_ref[:, c * LANES:(c + 1) * LANES] for c in range(n)]
        if mask is None:
            return chunks
        return [jnp.where(m, c, MASKED) for m, c in zip(_lane_chunks(mask), chunks)]
    return load


def _key_tile_sweep(qi, tq, tk, park, consume, first_step):
    every, lower = slice(0, tq), slice(tk, tq)
    park(0, 0, every)
    first_step()

    def body(i, carry):
        park(2 * i + 1, 1, every)
        consume(2 * i, 0, every, False)
        park(2 * i + 2, 0, every)
        consume(2 * i + 1, 1, every, False)
        return carry

    lax.fori_loop(0, qi, body, 0)
    park(2 * qi + 1, 1, lower)
    consume(2 * qi, 0, every, True)
    consume(2 * qi + 1, 1, lower, True)


def _diff_attn_kernel(lq1_ref, lk1_ref, lq2_ref, lk2_ref, g_ref, q_ref, k_ref, v_ref, km_ref, vm_ref,
                      o_ref, sh_ref, l_ref, acc_ref, s_ref, *, tq, tk, lambda_init):
    qi = pl.program_id(2)
    maps = [slice(c * HEAD_DIM, (c + 1) * HEAD_DIM) for c in range(2)]
    keys = lambda kj: pl.ds(pl.multiple_of(kj * tk, tk), tk)

    def chains(load_fn, v, rows):
        return [(load_fn(c, m), v, None, sh_ref.at[c, rows], l_ref.at[c, rows], acc_ref.at[c, rows])
                for c, m in enumerate(maps)]

    def park(kj, slot, rows):
        for c, m in enumerate(maps):
            s_ref[slot, c, rows] = _dot_nt(q_ref[rows, m], k_ref[keys(kj), m])

    def consume(kj, slot, rows, masked):
        mask = None
        if masked:
            n = rows.stop - rows.start
            row = lax.broadcasted_iota(jnp.int32, (n, tk), 0) + (tq - n - rows.start)
            col = lax.broadcasted_iota(jnp.int32, (n, tk), 1)
            mask = (col // CHUNK) <= (row // CHUNK)
        _softmax_step(chains(lambda c, m: _chunks_at(s_ref.at[slot, c, rows], mask),
                             v_ref[keys(kj), :], rows), False)

    def first_step():
        meta_ok = lax.broadcasted_iota(jnp.int32, (tq, META_PAD), 1) < N_META
        _softmax_step(chains(lambda c, m: _chunks_of(jnp.where(
            meta_ok, _dot_nt(q_ref[:, m], km_ref[:, m]), MASKED)), vm_ref[...], slice(0, tq)), True)

    _key_tile_sweep(qi, tq, tk, park, consume, first_step)

    lam = (jnp.exp(jnp.sum(lq1_ref[...] * lk1_ref[...], axis=1, keepdims=True))
           - jnp.exp(jnp.sum(lq2_ref[...] * lk2_ref[...], axis=1, keepdims=True))
           + lambda_init)
    o = jnp.concatenate(
        [acc_ref[0, :, m] / l_ref[0] - lam * (acc_ref[1, :, m] / l_ref[1]) for m in maps], axis=1)
    o = o * lax.rsqrt(jnp.mean(o * o, axis=-1, keepdims=True) + SUBLN_EPS)
    o_ref[...] = (o * g_ref[...] * (1.0 - lambda_init)).astype(o_ref.dtype)


def _diff_attention(proj, proj_meta, lam_vecs, subln_g, *, batch, seq, n_heads, lambda_init):
    ev = 2 * HEAD_DIM
    tq, tk = ATTN_TQ, ATTN_TK
    nq = seq // tq
    k_blk, v_blk = n_heads, 2 * n_heads
    vec = pl.BlockSpec((1, HEAD_DIM), lambda b, h, i: (0, 0))
    return pl.pallas_call(
        functools.partial(_diff_attn_kernel, tq=tq, tk=tk, lambda_init=lambda_init),
        grid=(batch, n_heads, nq),
        in_specs=[vec, vec, vec, vec,
                  pl.BlockSpec((1, ev), lambda b, h, i: (0, 0)),
                  pl.BlockSpec((tq, ev), lambda b, h, i: (b * nq + i, h)),
                  pl.BlockSpec((seq, ev), lambda b, h, i: (b, k_blk + h)),
                  pl.BlockSpec((seq, ev), lambda b, h, i: (b, v_blk + h)),
                  pl.BlockSpec((META_PAD, ev), lambda b, h, i: (0, k_blk + h)),
                  pl.BlockSpec((META_PAD, ev), lambda b, h, i: (0, v_blk + h))],
        out_specs=pl.BlockSpec((tq, ev), lambda b, h, i: (b * nq + i, h)),
        out_shape=jax.ShapeDtypeStruct((batch * seq, n_heads * ev), BF16),
        scratch_shapes=[pltpu.VMEM((2, tq, LANES), F32), pltpu.VMEM((2, tq, LANES), F32),
                        pltpu.VMEM((2, tq, ev), F32), pltpu.VMEM((2, 2, tq, tk), F32)],
        compiler_params=_params("arbitrary", "arbitrary", "arbitrary"),
        name="diff_attention",
    )(*lam_vecs, subln_g, proj, proj, proj, proj_meta, proj_meta)


def _fox_attn_kernel(q_ref, k_ref, v_ref, km_ref, vm_ref, cq_ref, ck_ref, cm_ref,
                     o_ref, sh_ref, acc_ref, s_ref, *, tq, tk):
    hp = pl.program_id(1)
    qi = pl.program_id(2)
    heads = [slice(c * HEAD_DIM, (c + 1) * HEAD_DIM) for c in range(2)]
    keys = lambda kj: pl.ds(pl.multiple_of(kj * tk, tk), tk)
    lane = lax.broadcasted_iota(jnp.int32, (tq, LANES), 1)
    cq = [LOG2E * jnp.sum(jnp.where(lane == 2 * hp + c, cq_ref[...], 0.0), axis=1, keepdims=True)
          for c in range(2)]
    ones = jnp.ones((tk, HEAD_DIM), BF16)

    def chains(load_fn, value_fn, ones_blk, rows):
        return [(load_fn(c, m), jnp.concatenate([value_fn(m), ones_blk], axis=1), cq[c][rows],
                 sh_ref.at[c, rows], None, acc_ref.at[c, rows]) for c, m in enumerate(heads)]

    def park(kj, slot, rows):
        for c, m in enumerate(heads):
            s_ref[slot, c, rows] = (_dot_nt(q_ref[rows, m], k_ref[keys(kj), m])
                                    - LOG2E * ck_ref[c, pl.ds(kj, 1), :])

    def consume(kj, slot, rows, masked):
        mask = None
        if masked:
            n = rows.stop - rows.start
            row = lax.broadcasted_iota(jnp.int32, (n, tk), 0) + (tq - n - rows.start)
            col = lax.broadcasted_iota(jnp.int32, (n, tk), 1)
            mask = col <= row
        _softmax_step(chains(lambda c, m: _chunks_at(s_ref.at[slot, c, rows], mask),
                             lambda m: v_ref[keys(kj), m], ones, rows), False)

    def first_step():
        meta_lane = lax.broadcasted_iota(jnp.int32, (1, META_PAD), 1)
        meta_ok = lax.broadcasted_iota(jnp.int32, (tq, META_PAD), 1) < N_META

        def meta_scores(c, m):
            cm = cm_ref[c]
            cm_last = jnp.sum(jnp.where(meta_lane == N_META - 1, cm, 0.0), axis=1, keepdims=True)
            return _chunks_of(jnp.where(
                meta_ok, _dot_nt(q_ref[:, m], km_ref[:, m]) - LOG2E * (cm - cm_last), MASKED))

        _softmax_step(chains(meta_scores, lambda m: vm_ref[:, m], ones[:META_PAD], slice(0, tq)), True)

    _key_tile_sweep(qi, tq, tk, park, consume, first_step)

    for c, m in enumerate(heads):
        o_ref[:, m] = (acc_ref[c, :, :HEAD_DIM] / acc_ref[c, :, HEAD_DIM:]).astype(o_ref.dtype)


def _fox_attention(proj, proj_meta, c_rows, c_lanes, cm_lanes, *, batch, seq, n_heads):
    tq, tk = ATTN_TQ, ATTN_TK
    nq = seq // tq
    pairs, ev = n_heads // 2, 2 * HEAD_DIM
    q_blk, k_blk, v_blk = 3 * pairs, 4 * pairs, 5 * pairs
    return pl.pallas_call(
        functools.partial(_fox_attn_kernel, tq=tq, tk=tk),
        grid=(batch, pairs, nq),
        in_specs=[pl.BlockSpec((tq, ev), lambda b, h, i: (b * nq + i, q_blk + h)),
                  pl.BlockSpec((seq, ev), lambda b, h, i: (b, k_blk + h)),
                  pl.BlockSpec((seq, ev), lambda b, h, i: (b, v_blk + h)),
                  pl.BlockSpec((META_PAD, ev), lambda b, h, i: (0, k_blk + h)),
                  pl.BlockSpec((META_PAD, ev), lambda b, h, i: (0, v_blk + h)),
                  pl.BlockSpec((tq, LANES), lambda b, h, i: (b * nq + i, 0)),
                  pl.BlockSpec((2, seq // tk, tk), lambda b, h, i: (b * pairs + h, 0, 0)),
                  pl.BlockSpec((2, 1, META_PAD), lambda b, h, i: (h, 0, 0))],
        out_specs=pl.BlockSpec((tq, ev), lambda b, h, i: (b * nq + i, h)),
        out_shape=jax.ShapeDtypeStruct((batch * seq, n_heads * HEAD_DIM), BF16),
        scratch_shapes=[pltpu.VMEM((2, tq, LANES), F32), pltpu.VMEM((2, tq, ev), F32),
                        pltpu.VMEM((2, 2, tq, tk), F32)],
        compiler_params=_params("arbitrary", "arbitrary", "arbitrary"),
        name="fox_attention",
    )(proj, proj, proj, proj_meta, proj_meta, c_rows, c_lanes, cm_lanes)


def _matmul_kernel(*refs, n_lhs, nk, epilogue):
    a_refs, w_refs, o_ref = refs[:n_lhs], refs[n_lhs:2 * n_lhs], refs[2 * n_lhs + 1]
    acc = _dot(a_refs[0][...], w_refs[0][...])
    for a_ref, w_ref in zip(a_refs[1:], w_refs[1:]):
        acc += _dot(a_ref[...], w_ref[...])
    if nk == 1:
        o_ref[...] = epilogue(acc).astype(o_ref.dtype)
        return
    acc_ref = refs[2 * n_lhs + 2]
    k = pl.program_id(2)

    @pl.when(k == 0)
    def _():
        acc_ref[...] = acc

    @pl.when(k > 0)
    def _():
        acc_ref[...] += acc

    @pl.when(k == nk - 1)
    def _():
        o_ref[...] = epilogue(acc_ref[...]).astype(o_ref.dtype)


def _matmul_head_kernel(*refs, n_lhs, nk, epilogue):
    a_refs, w_refs = refs[:n_lhs], refs[n_lhs:2 * n_lhs]
    o_ref, wb_refs = refs[2 * n_lhs], refs[2 * n_lhs + 1:3 * n_lhs + 1]
    acc = None
    for a_ref, w_ref, wb_ref in zip(a_refs, w_refs, wb_refs):
        wb = w_ref[...].astype(BF16)
        wb_ref[...] = wb
        acc = _dot(a_ref[...], wb) if acc is None else acc + _dot(a_ref[...], wb)
    if nk == 1:
        o_ref[...] = epilogue(acc).astype(o_ref.dtype)
        return
    acc_ref = refs[3 * n_lhs + 1]
    k = pl.program_id(1)

    @pl.when(k == 0)
    def _():
        acc_ref[...] = acc

    @pl.when(k > 0)
    def _():
        acc_ref[...] += acc

    @pl.when(k == nk - 1)
    def _():
        o_ref[...] = epilogue(acc_ref[...]).astype(o_ref.dtype)


def _matmul(lhs, w, out_dtype, epilogue=lambda acc: acc, name="matmul"):
    m, ka = lhs[0].shape
    n, n_lhs = w.shape[1], len(lhs)
    tm, tn, tk = _tile(m, 1024), _tile(n, 1024), _tile(ka, 4096 // n_lhs)
    nk = ka // tk
    tn_head = _tile(n, 512)
    out, *wb = pl.pallas_call(
        functools.partial(_matmul_head_kernel, n_lhs=n_lhs, nk=nk, epilogue=epilogue),
        grid=(n // tn_head, nk),
        in_specs=([pl.BlockSpec((tm, tk), lambda j, k: (0, k))] * n_lhs
                  + [pl.BlockSpec((tk, tn_head), lambda j, k, p=p: (p * nk + k, j)) for p in range(n_lhs)]),
        out_specs=([pl.BlockSpec((tm, tn_head), lambda j, k: (0, j))]
                   + [pl.BlockSpec((tk, tn_head), lambda j, k: (k, j))] * n_lhs),
        out_shape=([jax.ShapeDtypeStruct((m, n), out_dtype)]
                   + [jax.ShapeDtypeStruct((ka, n), BF16)] * n_lhs),
        scratch_shapes=[pltpu.VMEM((tm, tn_head), F32)] if nk > 1 else [],
        compiler_params=_params("arbitrary", "arbitrary"),
        name=name + "_head",
    )(*lhs, *([w] * n_lhs))
    if m == tm:
        return out
    return pl.pallas_call(
        functools.partial(_matmul_kernel, n_lhs=n_lhs, nk=nk, epilogue=epilogue),
        grid=(m // tm - 1, n // tn, nk),
        in_specs=([pl.BlockSpec((tm, tk), lambda i, j, k: (i + 1, k))] * n_lhs
                  + [pl.BlockSpec((tk, tn), lambda i, j, k: (k, j))] * n_lhs
                  + [pl.BlockSpec(memory_space=pl.ANY)]),
        out_specs=pl.BlockSpec((tm, tn), lambda i, j, k: (i + 1, j)),
        out_shape=jax.ShapeDtypeStruct((m, n), out_dtype),
        scratch_shapes=[pltpu.VMEM((tm, tn), F32)] if nk > 1 else [],
        input_output_aliases={2 * n_lhs: 0},
        compiler_params=_params("arbitrary", "arbitrary", "arbitrary"),
        name=name,
    )(*lhs, *wb, out)


def _attn_norm_kernel(x_ref, mix_ref, g0_ref, b0_ref, g_ref, b_ref, h_ref, hb_ref, *, alpha):
    h0 = _layer_norm(x_ref[...], g0_ref[...], b0_ref[...]).astype(x_ref.dtype)
    h1 = _layer_norm(alpha * h0 + mix_ref[...], g_ref[...], b_ref[...])
    h_ref[...] = h1
    hb_ref[...] = h1.astype(BF16)


def _mlp_norm_kernel(h_ref, ff_ref, g_ref, b_ref, o_ref, *, alpha):
    o_ref[...] = _layer_norm(alpha * h_ref[...] + ff_ref[...], g_ref[...], b_ref[...])


def _row_call(kernel_fn, row_inputs, vec_inputs, out_dtypes, name):
    m, d = row_inputs[0].shape
    tm = _tile(m, 256)
    rows = pl.BlockSpec((tm, d), lambda i: (i, 0))
    vec = pl.BlockSpec((1, d), lambda i: (0, 0))
    return pl.pallas_call(
        kernel_fn,
        grid=(m // tm,),
        in_specs=[rows] * len(row_inputs) + [vec] * len(vec_inputs),
        out_specs=[rows] * len(out_dtypes),
        out_shape=[jax.ShapeDtypeStruct((m, d), dt) for dt in out_dtypes],
        compiler_params=_params("arbitrary"),
        name=name,
    )(*row_inputs, *vec_inputs)


def _rope_tables(n):
    inv = 1.0 / (ROPE_THETA ** (jnp.arange(0, HEAD_DIM, 2, dtype=F32) / HEAD_DIM))
    ang = jnp.arange(n, dtype=F32)[:, None] * inv[None, :]
    ang = jnp.concatenate([ang, ang], axis=-1)
    sign = jnp.where(jnp.arange(HEAD_DIM) < HEAD_DIM // 2, -1.0, 1.0).astype(F32)
    return jnp.cos(ang), jnp.sin(ang) * sign


def kernel(x, meta_tokens, ln_in_g, ln_in_b, w_in, b_forget, lambda_q1, lambda_k1, lambda_q2, lambda_k2,
           subln_g, w_out, ln_attn_g, ln_attn_b, w_up, w_down, ln_mlp_g, ln_mlp_b):
    batch, seq, d = x.shape
    depth = w_in.shape[0]
    assert depth == 1, "meta-token outputs are not carried to a next layer"
    half = d // 2
    n_diff, n_fox = half // (2 * HEAD_DIM), half // HEAD_DIM
    ff_off = 6 * half
    assert w_in.shape[2] == ff_off + n_fox and n_fox <= LANES and n_fox % 2 == 0 and seq % CHUNK == 0
    alpha = (2.0 * depth) ** 0.25
    lambda_init = 0.8 - 0.6 * math.exp(-0.3 * 0)

    row = lambda v: v.reshape(1, -1).astype(F32)
    w_proj = w_in[0].astype(BF16)
    w_gate = jnp.pad(w_proj[:, ff_off:], ((0, 0), (0, LANES - n_fox)))
    b_gate = jnp.pad(b_forget[0], (0, LANES - n_fox)).reshape(1, LANES).astype(F32)
    g_in, b_in = row(ln_in_g), row(ln_in_b)

    cos, sin = _rope_tables(N_META + seq)
    pad_rows = lambda t: jnp.pad(t, ((0, META_PAD - N_META), (0, 0)))

    hm, cm = _ln_gate(pad_rows(meta_tokens.astype(F32)), g_in, b_in, w_gate, b_gate, seq_rows=META_PAD)
    proj_meta = _in_proj(hm, w_proj, pad_rows(cos[:N_META]), pad_rows(sin[:N_META]), seq_rows=META_PAD)

    x2 = x.reshape(batch * seq, d)
    h0b, c_rows = _ln_gate(x2, g_in, b_in, w_gate, b_gate, seq_rows=seq)
    proj = _in_proj(h0b, w_proj, cos[N_META:], sin[N_META:], seq_rows=seq)

    diff_out = _diff_attention(
        proj, proj_meta, [row(v[0]) for v in (lambda_q1, lambda_k1, lambda_q2, lambda_k2)],
        row(subln_g[0]), batch=batch, seq=seq, n_heads=n_diff, lambda_init=lambda_init)

    assert seq % ATTN_TQ == 0 and ATTN_TQ == 2 * ATTN_TK and ATTN_TK % CHUNK == 0
    c_lanes = (c_rows.reshape(batch, seq, LANES)[:, :, :n_fox].transpose(0, 2, 1)
               .reshape(batch * n_fox, seq // ATTN_TK, ATTN_TK))
    cm_lanes = cm[:, :n_fox].T.reshape(n_fox, 1, META_PAD)
    fox_out = _fox_attention(proj, proj_meta, c_rows, c_lanes, cm_lanes,
                             batch=batch, seq=seq, n_heads=n_fox)

    mix = _matmul([diff_out, fox_out], w_out[0], F32, name="out_proj")
    h1, h1b = _row_call(functools.partial(_attn_norm_kernel, alpha=alpha), [x2, mix],
                        [g_in, b_in, row(ln_attn_g[0]), row(ln_attn_b[0])], [F32, BF16], "attn_norm")

    act = _matmul([h1b], w_up[0], BF16,
                  epilogue=lambda acc: jnp.square(jnp.maximum(acc, 0.0)), name="mlp_up")
    ff = _matmul([act], w_down[0], F32, name="mlp_down")
    (out,) = _row_call(functools.partial(_mlp_norm_kernel, alpha=alpha), [h1, ff],
                       [row(ln_mlp_g[0]), row(ln_mlp_b[0])], [F32], "mlp_norm")
    return out.reshape(batch, seq, d)
```

```python
import functools
import math

import jax
import jax.numpy as jnp
from jax import lax
from jax.experimental import pallas as pl
from jax.experimental.pallas import tpu as pltpu

HEAD_DIM = 128
N_META = 16
CHUNK = 64
ROPE_THETA = 10000.0
LN_EPS = 1e-5
SUBLN_EPS = 1e-5

LANES = 128
VMEM_LIMIT_BYTES = 60 * 2**20

META_PAD = LANES
MASKED = -1e30
LOG2E = math.log2(math.e)
ATTN_TQ = 1024
ATTN_TK = 512

F32 = jnp.float32
BF16 = jnp.bfloat16


def _tile(dim, target):
    t = min(dim, target)
    while dim % t:
        t //= 2
    return t


def _params(*sem):
    return pltpu.CompilerParams(dimension_semantics=sem, vmem_limit_bytes=VMEM_LIMIT_BYTES)


def _dot(a, b):
    return jnp.dot(a, b, preferred_element_type=F32)


def _dot_nt(a, b):
    return lax.dot_general(a, b, (((1,), (1,)), ((), ())), preferred_element_type=F32)


def _layer_norm(x, g, b):
    mu = jnp.mean(x, axis=-1, keepdims=True)
    xc = x - mu
    var = jnp.mean(xc * xc, axis=-1, keepdims=True)
    return xc * lax.rsqrt(var + LN_EPS) * g + b


def _ln_gate_kernel(x_ref, g_ref, b_ref, wf_ref, bf_ref, h_ref, c_ref, carry_ref, *, tiles_per_seq):
    @pl.when(pl.program_id(0) % tiles_per_seq == 0)
    def _():
        carry_ref[...] = jnp.zeros_like(carry_ref)

    hb = _layer_norm(x_ref[...], g_ref[...], b_ref[...]).astype(BF16)
    h_ref[...] = hb
    z = _dot(hb, wf_ref[...]) + bf_ref[...]
    log_f = jnp.minimum(z, 0.0) - jnp.log1p(jnp.exp(-jnp.abs(z)))
    tm = z.shape[0]
    row = lax.broadcasted_iota(jnp.int32, (tm, tm), 0)
    col = lax.broadcasted_iota(jnp.int32, (tm, tm), 1)
    lower = (row >= col).astype(F32)
    csum = jnp.dot(lower, log_f, precision=lax.Precision.HIGHEST,
                   preferred_element_type=F32) + carry_ref[...]
    c_ref[...] = csum
    carry_ref[...] = csum[tm - 1:tm, :]


def _ln_gate(x2, g, b, wf, bf, *, seq_rows):
    m, d = x2.shape
    tm = _tile(seq_rows, 256)
    return pl.pallas_call(
        functools.partial(_ln_gate_kernel, tiles_per_seq=seq_rows // tm),
        grid=(m // tm,),
        in_specs=[pl.BlockSpec((tm, d), lambda i: (i, 0)),
                  pl.BlockSpec((1, d), lambda i: (0, 0)),
                  pl.BlockSpec((1, d), lambda i: (0, 0)),
                  pl.BlockSpec((d, LANES), lambda i: (0, 0)),
                  pl.BlockSpec((1, LANES), lambda i: (0, 0))],
        out_specs=[pl.BlockSpec((tm, d), lambda i: (i, 0)),
                   pl.BlockSpec((tm, LANES), lambda i: (i, 0))],
        out_shape=[jax.ShapeDtypeStruct((m, d), BF16), jax.ShapeDtypeStruct((m, LANES), F32)],
        scratch_shapes=[pltpu.VMEM((1, LANES), F32)],
        compiler_params=_params("arbitrary"),
        name="ln_gate",
    )(x2, g, b, wf, bf)


def _in_proj_kernel(a_ref, wt_ref, cos_ref, sin_ref, o_ref, *wtb_ref, tiles_per_region, scale):
    region = pl.program_id(1) // tiles_per_region
    tn, half = o_ref.shape[1], o_ref.shape[1] // 2
    if wtb_ref:
        wtb_ref[0][...] = wt_ref[...].astype(BF16)
        wt_ref = wtb_ref[0]

    def halves():
        for cols in (slice(0, half), slice(half, tn)):
            yield cols, _dot_nt(a_ref[...], wt_ref[cols, :])

    def rope(mult):
        cos = cos_ref[...] * mult
        sin = sin_ref[...] * mult
        for cols, acc in halves():
            for c in range(half // HEAD_DIM):
                t = acc[:, c * HEAD_DIM:(c + 1) * HEAD_DIM]
                lo = cols.start + c * HEAD_DIM
                o_ref[:, lo:lo + HEAD_DIM] = (
                    t * cos + pltpu.roll(t, HEAD_DIM // 2, 1) * sin).astype(o_ref.dtype)

    def plain(mult):
        for cols, acc in halves():
            o_ref[:, cols] = (acc if mult is None else acc * mult).astype(o_ref.dtype)

    @pl.when(region == 0)
    def _():
        rope(scale)

    @pl.when(region == 1)
    def _():
        rope(1.0)

    @pl.when(region == 3)
    def _():
        plain(scale)

    @pl.when((region == 2) | (region >= 4))
    def _():
        plain(None)


def _in_proj(h, wt, cos, sin, *, seq_rows):
    m, d = h.shape
    region = d // 2
    n = 6 * region
    emit_bf16 = wt.dtype != BF16
    tm = _tile(seq_rows, 1024)
    tn = _tile(region, 512 if emit_bf16 else 1024)
    tiles_per_seq = seq_rows // tm
    w_spec = pl.BlockSpec((tn, d), lambda i, j: (j, 0))
    out = pl.pallas_call(
        functools.partial(_in_proj_kernel, tiles_per_region=region // tn,
                          scale=LOG2E * HEAD_DIM ** -0.5),
        grid=(m // tm, n // tn),
        in_specs=[pl.BlockSpec((tm, d), lambda i, j: (i, 0)),
                  w_spec,
                  pl.BlockSpec((tm, HEAD_DIM), lambda i, j: (i % tiles_per_seq, 0)),
                  pl.BlockSpec((tm, HEAD_DIM), lambda i, j: (i % tiles_per_seq, 0))],
        out_specs=[pl.BlockSpec((tm, tn), lambda i, j: (i, j))] + [w_spec] * emit_bf16,
        out_shape=([jax.ShapeDtypeStruct((m, n), BF16)]
                   + [jax.ShapeDtypeStruct((n, d), BF16)] * emit_bf16),
        compiler_params=_params("arbitrary", "arbitrary"),
        name="in_proj",
    )(h, wt, cos, sin)
    return out if emit_bf16 else out[0]


def _lane_chunks(a):
    return [a[:, c * LANES:(c + 1) * LANES] for c in range(a.shape[1] // LANES)]


def _softmax_step(chains, first):
    staged = []
    for load_s, v, rb, sh_ref, l_ref, acc_ref in chains:
        smax = jnp.max(functools.reduce(jnp.maximum, load_s()), axis=1, keepdims=True)
        if first:
            shift = jnp.broadcast_to(smax if rb is None else (smax + rb) - rb, sh_ref.shape)
            alpha = None
        else:
            old = sh_ref[...]
            shift = jnp.maximum(old, smax) if rb is None else jnp.maximum(old + rb, smax + rb) - rb
            alpha = jnp.exp2(old - shift)
        sh_ref[...] = shift
        staged.append((shift, alpha))
    for (load_s, v, rb, sh_ref, l_ref, acc_ref), (shift, alpha) in zip(chains, staged):
        p = [jnp.exp2(c - shift) for c in load_s()]
        pv = _dot(jnp.concatenate(p, axis=1).astype(v.dtype), v)
        if l_ref is not None:
            psum = jnp.sum(functools.reduce(jnp.add, p), axis=1, keepdims=True)
            l_ref[...] = jnp.broadcast_to(psum, l_ref.shape) if first else alpha * l_ref[...] + psum
        for c in range(pv.shape[1] // LANES):
            cols = slice(c * LANES, (c + 1) * LANES)
            acc_ref[:, cols] = pv[:, cols] if first else alpha * acc_ref[:, cols] + pv[:, cols]


def _chunks_of(s):
    return lambda: _lane_chunks(s)


def _chunks_at(s_ref, mask=None):
    def load():
        n = s_ref.shape[1] // LANES
        chunks = [s_ref[:, c * LANES:(c + 1) * LANES] for c in range(n)]
        if mask is None:
            return chunks
        return [jnp.where(m, c, MASKED) for m, c in zip(_lane_chunks(mask), chunks)]
    return load


def _key_tile_sweep(qi, tq, tk, park, consume, first_step):
    every, lower = slice(0, tq), slice(tk, tq)
    park(0, 0, every)
    first_step()

    def body(i, carry):
        park(2 * i + 1, 1, every)
        consume(2 * i, 0, every, False)
        park(2 * i + 2, 0, every)
        consume(2 * i + 1, 1, every, False)
        return carry

    lax.fori_loop(0, qi, body, 0)
    park(2 * qi + 1, 1, lower)
    consume(2 * qi, 0, every, True)
    consume(2 * qi + 1, 1, lower, True)


def _diff_attn_kernel(lq1_ref, lk1_ref, lq2_ref, lk2_ref, g_ref, q_ref, k_ref, v_ref, km_ref, vm_ref,
                      *refs, n_cast, tq, tk, lambda_init):
    cast_src, (o_ref, *cast_dst) = refs[:n_cast], refs[n_cast:2 * n_cast + 1]
    sh_ref, l_ref, acc_ref, s_ref = refs[2 * n_cast + 1:]
    for src, dst in zip(cast_src, cast_dst):
        dst[...] = src[...].astype(BF16)
    qi = pl.program_id(2)
    maps = [slice(c * HEAD_DIM, (c + 1) * HEAD_DIM) for c in range(2)]
    keys = lambda kj: pl.ds(pl.multiple_of(kj * tk, tk), tk)

    def chains(load_fn, v, rows):
        return [(load_fn(c, m), v, None, sh_ref.at[c, rows], l_ref.at[c, rows], acc_ref.at[c, rows])
                for c, m in enumerate(maps)]

    def park(kj, slot, rows):
        for c, m in enumerate(maps):
            s_ref[slot, c, rows] = _dot_nt(q_ref[rows, m], k_ref[keys(kj), m])

    def consume(kj, slot, rows, masked):
        mask = None
        if masked:
            n = rows.stop - rows.start
            row = lax.broadcasted_iota(jnp.int32, (n, tk), 0) + (tq - n - rows.start)
            col = lax.broadcasted_iota(jnp.int32, (n, tk), 1)
            mask = (col // CHUNK) <= (row // CHUNK)
        _softmax_step(chains(lambda c, m: _chunks_at(s_ref.at[slot, c, rows], mask),
                             v_ref[keys(kj), :], rows), False)

    def first_step():
        meta_ok = lax.broadcasted_iota(jnp.int32, (tq, META_PAD), 1) < N_META
        _softmax_step(chains(lambda c, m: _chunks_of(jnp.where(
            meta_ok, _dot_nt(q_ref[:, m], km_ref[:, m]), MASKED)), vm_ref[...], slice(0, tq)), True)

    _key_tile_sweep(qi, tq, tk, park, consume, first_step)

    lam = (jnp.exp(jnp.sum(lq1_ref[...] * lk1_ref[...], axis=1, keepdims=True))
           - jnp.exp(jnp.sum(lq2_ref[...] * lk2_ref[...], axis=1, keepdims=True))
           + lambda_init)
    o = jnp.concatenate(
        [acc_ref[0, :, m] / l_ref[0] - lam * (acc_ref[1, :, m] / l_ref[1]) for m in maps], axis=1)
    o = o * lax.rsqrt(jnp.mean(o * o, axis=-1, keepdims=True) + SUBLN_EPS)
    o_ref[...] = (o * g_ref[...] * (1.0 - lambda_init)).astype(o_ref.dtype)


def _diff_attention(proj, proj_meta, lam_vecs, subln_g, to_bf16, *, batch, seq, n_heads, lambda_init):
    ev = 2 * HEAD_DIM
    tq, tk = ATTN_TQ, ATTN_TK
    nq = seq // tq
    steps = batch * n_heads * nq
    slab = lambda w: pl.BlockSpec((w.shape[0] // steps, w.shape[1]),
                                  lambda b, h, i: ((b * n_heads + h) * nq + i, 0))
    assert all(w.shape[0] % (steps * 16) == 0 for w in to_bf16)
    k_blk, v_blk = n_heads, 2 * n_heads
    vec = pl.BlockSpec((1, HEAD_DIM), lambda b, h, i: (0, 0))
    return pl.pallas_call(
        functools.partial(_diff_attn_kernel, n_cast=len(to_bf16), tq=tq, tk=tk, lambda_init=lambda_init),
        grid=(batch, n_heads, nq),
        in_specs=[vec, vec, vec, vec,
                  pl.BlockSpec((1, ev), lambda b, h, i: (0, 0)),
                  pl.BlockSpec((tq, ev), lambda b, h, i: (b * nq + i, h)),
                  pl.BlockSpec((seq, ev), lambda b, h, i: (b, k_blk + h)),
                  pl.BlockSpec((seq, ev), lambda b, h, i: (b, v_blk + h)),
                  pl.BlockSpec((META_PAD, ev), lambda b, h, i: (0, k_blk + h)),
                  pl.BlockSpec((META_PAD, ev), lambda b, h, i: (0, v_blk + h))]
                 + [slab(w) for w in to_bf16],
        out_specs=[pl.BlockSpec((tq, ev), lambda b, h, i: (b * nq + i, h))] + [slab(w) for w in to_bf16],
        out_shape=([jax.ShapeDtypeStruct((batch * seq, n_heads * ev), BF16)]
                   + [jax.ShapeDtypeStruct(w.shape, BF16) for w in to_bf16]),
        scratch_shapes=[pltpu.VMEM((2, tq, LANES), F32), pltpu.VMEM((2, tq, LANES), F32),
                        pltpu.VMEM((2, tq, ev), F32), pltpu.VMEM((2, 2, tq, tk), F32)],
        compiler_params=_params("arbitrary", "arbitrary", "arbitrary"),
        name="diff_attention",
    )(*lam_vecs, subln_g, proj, proj, proj, proj_meta, proj_meta, *to_bf16)


def _fox_attn_kernel(q_ref, k_ref, v_ref, km_ref, vm_ref, cq_ref, ck_ref, cm_ref,
                     o_ref, sh_ref, acc_ref, s_ref, *, tq, tk):
    hp = pl.program_id(1)
    qi = pl.program_id(2)
    heads = [slice(c * HEAD_DIM, (c + 1) * HEAD_DIM) for c in range(2)]
    keys = lambda kj: pl.ds(pl.multiple_of(kj * tk, tk), tk)
    lane = lax.broadcasted_iota(jnp.int32, (tq, LANES), 1)
    cq = [LOG2E * jnp.sum(jnp.where(lane == 2 * hp + c, cq_ref[...], 0.0), axis=1, keepdims=True)
          for c in range(2)]
    ones = jnp.ones((tk, HEAD_DIM), BF16)

    def chains(load_fn, value_fn, ones_blk, rows):
        return [(load_fn(c, m), jnp.concatenate([value_fn(m), ones_blk], axis=1), cq[c][rows],
                 sh_ref.at[c, rows], None, acc_ref.at[c, rows]) for c, m in enumerate(heads)]

    def park(kj, slot, rows):
        for c, m in enumerate(heads):
            s_ref[slot, c, rows] = (_dot_nt(q_ref[rows, m], k_ref[keys(kj), m])
                                    - LOG2E * ck_ref[c, pl.ds(kj, 1), :])

    def consume(kj, slot, rows, masked):
        mask = None
        if masked:
            n = rows.stop - rows.start
            row = lax.broadcasted_iota(jnp.int32, (n, tk), 0) + (tq - n - rows.start)
            col = lax.broadcasted_iota(jnp.int32, (n, tk), 1)
            mask = col <= row
        _softmax_step(chains(lambda c, m: _chunks_at(s_ref.at[slot, c, rows], mask),
                             lambda m: v_ref[keys(kj), m], ones, rows), False)

    def first_step():
        meta_lane = lax.broadcasted_iota(jnp.int32, (1, META_PAD), 1)
        meta_ok = lax.broadcasted_iota(jnp.int32, (tq, META_PAD), 1) < N_META

        def meta_scores(c, m):
            cm = cm_ref[c]
            cm_last = jnp.sum(jnp.where(meta_lane == N_META - 1, cm, 0.0), axis=1, keepdims=True)
            return _chunks_of(jnp.where(
                meta_ok, _dot_nt(q_ref[:, m], km_ref[:, m]) - LOG2E * (cm - cm_last), MASKED))

        _softmax_step(chains(meta_scores, lambda m: vm_ref[:, m], ones[:META_PAD], slice(0, tq)), True)

    _key_tile_sweep(qi, tq, tk, park, consume, first_step)

    for c, m in enumerate(heads):
        o_ref[:, m] = (acc_ref[c, :, :HEAD_DIM] / acc_ref[c, :, HEAD_DIM:]).astype(o_ref.dtype)


def _fox_attention(proj, proj_meta, c_rows, c_lanes, cm_lanes, *, batch, seq, n_heads):
    tq, tk = ATTN_TQ, ATTN_TK
    nq = seq // tq
    pairs, ev = n_heads // 2, 2 * HEAD_DIM
    q_blk, k_blk, v_blk = 3 * pairs, 4 * pairs, 5 * pairs
    return pl.pallas_call(
        functools.partial(_fox_attn_kernel, tq=tq, tk=tk),
        grid=(batch, pairs, nq),
        in_specs=[pl.BlockSpec((tq, ev), lambda b, h, i: (b * nq + i, q_blk + h)),
                  pl.BlockSpec((seq, ev), lambda b, h, i: (b, k_blk + h)),
                  pl.BlockSpec((seq, ev), lambda b, h, i: (b, v_blk + h)),
                  pl.BlockSpec((META_PAD, ev), lambda b, h, i: (0, k_blk + h)),
                  pl.BlockSpec((META_PAD, ev), lambda b, h, i: (0, v_blk + h)),
                  pl.BlockSpec((tq, LANES), lambda b, h, i: (b * nq + i, 0)),
                  pl.BlockSpec((2, seq // tk, tk), lambda b, h, i: (b * pairs + h, 0, 0)),
                  pl.BlockSpec((2, 1, META_PAD), lambda b, h, i: (h, 0, 0))],
        out_specs=pl.BlockSpec((tq, ev), lambda b, h, i: (b * nq + i, h)),
        out_shape=jax.ShapeDtypeStruct((batch * seq, n_heads * HEAD_DIM), BF16),
        scratch_shapes=[pltpu.VMEM((2, tq, LANES), F32), pltpu.VMEM((2, tq, ev), F32),
                        pltpu.VMEM((2, 2, tq, tk), F32)],
        compiler_params=_params("arbitrary", "arbitrary", "arbitrary"),
        name="fox_attention",
    )(proj, proj, proj, proj_meta, proj_meta, c_rows, c_lanes, cm_lanes)


def _matmul_kernel(*refs, n_lhs, nk, epilogue):
    a_refs, w_refs, o_ref = refs[:n_lhs], refs[n_lhs:2 * n_lhs], refs[2 * n_lhs]
    acc = _dot(a_refs[0][...], w_refs[0][...])
    for a_ref, w_ref in zip(a_refs[1:], w_refs[1:]):
        acc += _dot(a_ref[...], w_ref[...])
    if nk == 1:
        o_ref[...] = epilogue(acc).astype(o_ref.dtype)
        return
    acc_ref = refs[2 * n_lhs + 1]
    k = pl.program_id(2)

    @pl.when(k == 0)
    def _():
        acc_ref[...] = acc

    @pl.when(k > 0)
    def _():
        acc_ref[...] += acc

    @pl.when(k == nk - 1)
    def _():
        o_ref[...] = epilogue(acc_ref[...]).astype(o_ref.dtype)


def _matmul(lhs, w, out_dtype, epilogue=lambda acc: acc, name="matmul"):
    m, ka = lhs[0].shape
    n = w.shape[1]
    tm, tn, tk = _tile(m, 1024), _tile(n, 1024), _tile(ka, 4096 // len(lhs))
    nk = ka // tk
    w_spec = lambda p: pl.BlockSpec((tk, tn), lambda i, j, k: (p * nk + k, j))
    return pl.pallas_call(
        functools.partial(_matmul_kernel, n_lhs=len(lhs), nk=nk, epilogue=epilogue),
        grid=(m // tm, n // tn, nk),
        in_specs=([pl.BlockSpec((tm, tk), lambda i, j, k: (i, k))] * len(lhs)
                  + [w_spec(p) for p in range(len(lhs))]),
        out_specs=pl.BlockSpec((tm, tn), lambda i, j, k: (i, j)),
        out_shape=jax.ShapeDtypeStruct((m, n), out_dtype),
        scratch_shapes=[pltpu.VMEM((tm, tn), F32)] if nk > 1 else [],
        compiler_params=_params("arbitrary", "arbitrary", "arbitrary"),
        name=name,
    )(*lhs, *([w] * len(lhs)))


def _attn_norm_kernel(x_ref, mix_ref, g0_ref, b0_ref, g_ref, b_ref, h_ref, hb_ref, *, alpha):
    h0 = _layer_norm(x_ref[...], g0_ref[...], b0_ref[...]).astype(x_ref.dtype)
    h1 = _layer_norm(alpha * h0 + mix_ref[...], g_ref[...], b_ref[...])
    h_ref[...] = h1
    hb_ref[...] = h1.astype(BF16)


def _mlp_norm_kernel(h_ref, ff_ref, g_ref, b_ref, o_ref, *, alpha):
    o_ref[...] = _layer_norm(alpha * h_ref[...] + ff_ref[...], g_ref[...], b_ref[...])


def _row_call(kernel_fn, row_inputs, vec_inputs, out_dtypes, name):
    m, d = row_inputs[0].shape
    tm = _tile(m, 256)
    rows = pl.BlockSpec((tm, d), lambda i: (i, 0))
    vec = pl.BlockSpec((1, d), lambda i: (0, 0))
    return pl.pallas_call(
        kernel_fn,
        grid=(m // tm,),
        in_specs=[rows] * len(row_inputs) + [vec] * len(vec_inputs),
        out_specs=[rows] * len(out_dtypes),
        out_shape=[jax.ShapeDtypeStruct((m, d), dt) for dt in out_dtypes],
        compiler_params=_params("arbitrary"),
        name=name,
    )(*row_inputs, *vec_inputs)


def _rope_tables(n):
    inv = 1.0 / (ROPE_THETA ** (jnp.arange(0, HEAD_DIM, 2, dtype=F32) / HEAD_DIM))
    ang = jnp.arange(n, dtype=F32)[:, None] * inv[None, :]
    ang = jnp.concatenate([ang, ang], axis=-1)
    sign = jnp.where(jnp.arange(HEAD_DIM) < HEAD_DIM // 2, -1.0, 1.0).astype(F32)
    return jnp.cos(ang), jnp.sin(ang) * sign


def kernel(x, meta_tokens, ln_in_g, ln_in_b, w_in, b_forget, lambda_q1, lambda_k1, lambda_q2, lambda_k2,
           subln_g, w_out, ln_attn_g, ln_attn_b, w_up, w_down, ln_mlp_g, ln_mlp_b):
    batch, seq, d = x.shape
    depth = w_in.shape[0]
    assert depth == 1, "meta-token outputs are not carried to a next layer"
    half = d // 2
    n_diff, n_fox = half // (2 * HEAD_DIM), half // HEAD_DIM
    ff_off = 6 * half
    assert w_in.shape[2] == ff_off + n_fox and n_fox <= LANES and n_fox % 2 == 0 and seq % CHUNK == 0
    alpha = (2.0 * depth) ** 0.25
    lambda_init = 0.8 - 0.6 * math.exp(-0.3 * 0)

    row = lambda v: v.reshape(1, -1).astype(F32)
    wt_in = jnp.swapaxes(w_in[0], 0, 1)
    w_gate = jnp.pad(wt_in[ff_off:].T, ((0, 0), (0, LANES - n_fox))).astype(BF16)
    b_gate = jnp.pad(b_forget[0], (0, LANES - n_fox)).reshape(1, LANES).astype(F32)
    g_in, b_in = row(ln_in_g), row(ln_in_b)

    cos, sin = _rope_tables(N_META + seq)
    pad_rows = lambda t: jnp.pad(t, ((0, META_PAD - N_META), (0, 0)))

    hm, cm = _ln_gate(pad_rows(meta_tokens.astype(F32)), g_in, b_in, w_gate, b_gate, seq_rows=META_PAD)
    proj_meta, wt_bf16 = _in_proj(hm, wt_in, pad_rows(cos[:N_META]), pad_rows(sin[:N_META]),
                                  seq_rows=META_PAD)

    x2 = x.reshape(batch * seq, d)
    h0b, c_rows = _ln_gate(x2, g_in, b_in, w_gate, b_gate, seq_rows=seq)
    proj = _in_proj(h0b, wt_bf16, cos[N_META:], sin[N_META:], seq_rows=seq)

    diff_out, w_o, w_u, w_d = _diff_attention(
        proj, proj_meta, [row(v[0]) for v in (lambda_q1, lambda_k1, lambda_q2, lambda_k2)],
        row(subln_g[0]), [w_out[0], w_up[0], w_down[0]],
        batch=batch, seq=seq, n_heads=n_diff, lambda_init=lambda_init)

    assert seq % ATTN_TQ == 0 and ATTN_TQ == 2 * ATTN_TK and ATTN_TK % CHUNK == 0
    c_lanes = (c_rows.reshape(batch, seq, LANES)[:, :, :n_fox].transpose(0, 2, 1)
               .reshape(batch * n_fox, seq // ATTN_TK, ATTN_TK))
    cm_lanes = cm[:, :n_fox].T.reshape(n_fox, 1, META_PAD)
    fox_out = _fox_attention(proj, proj_meta, c_rows, c_lanes, cm_lanes,
                             batch=batch, seq=seq, n_heads=n_fox)

    mix = _matmul([diff_out, fox_out], w_o, F32, name="out_proj")
    h1, h1b = _row_call(functools.partial(_attn_norm_kernel, alpha=alpha), [x2, mix],
                        [g_in, b_in, row(ln_attn_g[0]), row(ln_attn_b[0])], [F32, BF16], "attn_norm")

    act = _matmul([h1b], w_u, BF16,
                  epilogue=lambda acc: jnp.square(jnp.maximum(acc, 0.0)), name="mlp_up")
    ff = _matmul([act], w_d, F32, name="mlp_down")
    (out,) = _row_call(functools.partial(_mlp_norm_kernel, alpha=alpha), [h1, ff],
                       [row(ln_mlp_g[0]), row(ln_mlp_b[0])], [F32], "mlp_norm")
    return out.reshape(batch, seq, d)
```
